```python
import jax, jax.numpy as jnp
from jax import lax
import numpy as np

D_MODEL = 2048
BATCH = 1
SEQ = 16384
DEPTH = 2

BRANCH_W = D_MODEL // 2
N_BRANCH = 3
EPS = 1e-6
CONV_W = 4
LRU_BLOCKS = 16
LRU_BLOCK = BRANCH_W // LRU_BLOCKS
LRU_C = 8.0
GDN_HEADS = 8
GDN_HEAD_DIM = BRANCH_W // GDN_HEADS
GDN_CHUNK = 64
RWKV_HEAD = 64
RWKV_HEADS = BRANCH_W // RWKV_HEAD
RWKV_W_RANK = 64
RWKV_A_RANK = 64
RWKV_G_RANK = 160
RWKV_GN_EPS = 64e-5
RWKV_COLS = 3 * BRANCH_W + RWKV_W_RANK + RWKV_A_RANK + RWKV_G_RANK
D_FF = ((8 * D_MODEL // 3 + 255) // 256) * 256
IN_SIZES = (BRANCH_W, BRANCH_W,
            3 * BRANCH_W, BRANCH_W, GDN_HEADS, GDN_HEADS,
            RWKV_COLS,
            N_BRANCH * D_MODEL)
N_IN = sum(IN_SIZES)

kernel_name = 'hybrid_rglru_gdn_rwkv7_block'


def split_cols(t, sizes):
    idx = [int(i) for i in np.cumsum(sizes)[:-1]]
    return jnp.split(t, idx, axis=-1)


def rms_norm(x, w):
    xf = x.astype(jnp.float32)
    y = xf * lax.rsqrt(jnp.mean(xf * xf, axis=-1, keepdims=True) + EPS)
    return (y * w.astype(jnp.float32)).astype(x.dtype)


def l2_normalize(t):
    return t * lax.rsqrt(jnp.sum(t * t, axis=-1, keepdims=True) + EPS)


def causal_depthwise_conv(x, w, b=None):
    k = w.shape[0]
    xp = jnp.pad(x, ((0, 0), (k - 1, 0), (0, 0)))
    y = lax.conv_general_dilated(xp, w[:, None, :].astype(x.dtype), window_strides=(1,), padding='VALID',
                                 dimension_numbers=('NWC', 'WIO', 'NWC'), feature_group_count=x.shape[-1])
    return y if b is None else y + b.astype(x.dtype)


def rglru_branch(x_in, gate_in, conv_w, conv_b, w_a, b_a, w_x, b_x, lam):
    dtype = x_in.dtype
    xc = causal_depthwise_conv(x_in, conv_w, conv_b).astype(jnp.float32)
    B, S, C = xc.shape
    xb = xc.reshape(B, S, LRU_BLOCKS, LRU_BLOCK)
    r = jax.nn.sigmoid(jnp.einsum('bsgi,gij->bsgj', xb, w_a.astype(jnp.float32)).reshape(B, S, C) + b_a)
    i = jax.nn.sigmoid(jnp.einsum('bsgi,gij->bsgj', xb, w_x.astype(jnp.float32)).reshape(B, S, C) + b_x)
    log_a = -LRU_C * r * jax.nn.softplus(-lam.astype(jnp.float32))
    a = jnp.exp(log_a)
    u = jnp.sqrt(-jnp.expm1(2.0 * log_a)) * (i * xc)

    def combine(left, right):
        a_l, b_l = left
        a_r, b_r = right
        return a_l * a_r, a_r * b_l + b_r

    _, h = lax.associative_scan(combine, (a, u), axis=1)
    return (jax.nn.gelu(gate_in.astype(jnp.float32)) * h).astype(dtype)


def chunk_gated_delta_rule(q, k, v, g, beta):
    B, S, H, Dk = q.shape
    Dv = v.shape[-1]
    C = GDN_CHUNK
    N = S // C

    def to_chunks(t):
        return jnp.moveaxis(t.reshape(B, N, C, H, -1), 3, 1)

    q, k, v = to_chunks(q), to_chunks(k), to_chunks(v)
    g = to_chunks(g[..., None])[..., 0]
    beta = to_chunks(beta[..., None])[..., 0]
    gc = jnp.cumsum(g, axis=-1)
    causal = jnp.tril(jnp.ones((C, C), bool))
    strict = jnp.tril(jnp.ones((C, C), bool), -1)
    decay = jnp.exp(jnp.where(causal, gc[..., :, None] - gc[..., None, :], -jnp.inf))
    kk = jnp.einsum('bhnid,bhnjd->bhnij', k, k)
    m = jnp.where(strict, beta[..., :, None] * kk * decay, 0.0)
    a_mat = m + jnp.eye(C, dtype=m.dtype)
    rhs = jnp.concatenate([beta[..., None] * v, (beta * jnp.exp(gc))[..., None] * k], axis=-1)
    sol = lax.linalg.triangular_solve(a_mat, rhs, left_side=True, lower=True, unit_diagonal=True)
    u_v, w_k = sol[..., :Dv], sol[..., Dv:]
    attn = jnp.where(causal, jnp.einsum('bhnid,bhnjd->bhnij', q, k) * decay, 0.0)
    q_dec = q * jnp.exp(gc)[..., None]
    k_dec = k * jnp.exp(gc[..., -1:] - gc)[..., None]
    g_last = jnp.exp(gc[..., -1])
    xs = tuple(jnp.moveaxis(t, 2, 0) for t in (u_v, w_k, attn, q_dec, k_dec, g_last))

    def step(state, inp):
        u_v_c, w_k_c, attn_c, q_c, k_c, gl = inp
        u = u_v_c - jnp.einsum('bhck,bhkv->bhcv', w_k_c, state)
        o = jnp.einsum('bhck,bhkv->bhcv', q_c, state) + jnp.einsum('bhij,bhjv->bhiv', attn_c, u)
        state = state * gl[..., None, None] + jnp.einsum('bhck,bhcv->bhkv', k_c, u)
        return state, o

    s0 = jnp.zeros((B, H, Dk, Dv), q.dtype)
    _, o = lax.scan(step, s0, xs)
    return jnp.transpose(o, (1, 0, 3, 2, 4)).reshape(B, S, H, Dv)


def gated_deltanet_branch(qkv, z, beta_raw, alpha_raw, conv_w, a_log, dt_bias, norm_w):
    dtype = qkv.dtype
    B, S, _ = qkv.shape
    qkv = jax.nn.silu(causal_depthwise_conv(qkv, conv_w).astype(jnp.float32))
    q, k, v = (t.reshape(B, S, GDN_HEADS, GDN_HEAD_DIM) for t in jnp.split(qkv, 3, axis=-1))
    q = l2_normalize(q) * (GDN_HEAD_DIM ** -0.5)
    k = l2_normalize(k)
    beta = jax.nn.sigmoid(beta_raw.astype(jnp.float32))
    g = -jnp.exp(a_log.astype(jnp.float32)) * jax.nn.softplus(alpha_raw.astype(jnp.float32) + dt_bias.astype(jnp.float32))
    o = chunk_gated_delta_rule(q, k, v, g, beta)
    o = o * lax.rsqrt(jnp.mean(o * o, axis=-1, keepdims=True) + EPS) * norm_w.astype(jnp.float32)
    o = o * jax.nn.silu(z.astype(jnp.float32)).reshape(B, S, GDN_HEADS, GDN_HEAD_DIM)
    return o.reshape(B, S, BRANCH_W).astype(dtype)


def rwkv7_branch(slab, mu, w0, w_up, a0, a_up, g_up, k_k, k_a, r_k, ln_w, ln_b):
    dtype = slab.dtype
    f32 = jnp.float32
    s = slab.astype(f32)
    prev = jnp.pad(s, ((0, 0), (1, 0), (0, 0)))[:, :-1]
    s = s + (prev - s) * mu.astype(f32)
    r, k, v, w_lo, a_lo, g_lo = split_cols(s, (BRANCH_W, BRANCH_W, BRANCH_W, RWKV_W_RANK, RWKV_A_RANK, RWKV_G_RANK))
    B, S, _ = r.shape
    w = -jax.nn.softplus(-(w0.astype(f32) + jnp.tanh(w_lo) @ w_up.astype(f32))) - 0.5
    a = jax.nn.sigmoid(a0.astype(f32) + a_lo @ a_up.astype(f32))
    g = jax.nn.sigmoid(g_lo) @ g_up.astype(f32)

    def heads(t):
        return t.reshape(B, S, RWKV_HEADS, RWKV_HEAD)

    kk = l2_normalize(heads(k * k_k.astype(f32)))
    k = k * (1.0 + (a - 1.0) * k_a.astype(f32))
    r_h, k_h, v_h, a_h = heads(r), heads(k), heads(v), heads(a)
    dec = heads(jnp.exp(-jnp.exp(w)))
    xs = tuple(jnp.moveaxis(t, 1, 0) for t in (r_h, dec, k_h, v_h, kk, a_h))

    def step(state, inp):
        r_t, d_t, k_t, v_t, kk_t, a_t = inp
        sa = jnp.einsum('bhvk,bhk->bhv', state, -kk_t)
        state = (state * d_t[:, :, None, :] + sa[..., None] * (kk_t * a_t)[:, :, None, :]
                 + v_t[..., None] * k_t[:, :, None, :])
        return state, jnp.einsum('bhvk,bhk->bhv', state, r_t)

    s0 = jnp.zeros((B, RWKV_HEADS, RWKV_HEAD, RWKV_HEAD), f32)
    _, out = lax.scan(step, s0, xs)
    out = jnp.moveaxis(out, 0, 1)
    mean = jnp.mean(out, axis=-1, keepdims=True)
    var = jnp.mean(jnp.square(out - mean), axis=-1, keepdims=True)
    out = ((out - mean) * lax.rsqrt(var + RWKV_GN_EPS)).reshape(B, S, BRANCH_W) * ln_w.astype(f32) + ln_b.astype(f32)
    bonus = jnp.sum(r_h * k_h * r_k.astype(f32), axis=-1, keepdims=True) * v_h
    return ((out + bonus.reshape(B, S, BRANCH_W)) * g).astype(dtype)


def hybrid_mixer(h, w_in, lru_conv_w, lru_conv_b, lru_w_a, lru_b_a, lru_w_x, lru_b_x, lru_lambda,
                 gdn_conv_w, gdn_a_log, gdn_dt_bias, gdn_norm_w,
                 rwkv_mu, rwkv_w0, rwkv_w_up, rwkv_a0, rwkv_a_up, rwkv_g_up, rwkv_k_k, rwkv_k_a, rwkv_r_k,
                 rwkv_ln_w, rwkv_ln_b, w_branch, w_out):
    B, S, D = h.shape
    p = h @ w_in
    lru_x, lru_g, qkv, z, beta_raw, alpha_raw, rw, gate_raw = split_cols(p, IN_SIZES)
    y_a = rglru_branch(lru_x, lru_g, lru_conv_w, lru_conv_b, lru_w_a, lru_b_a, lru_w_x, lru_b_x, lru_lambda)
    y_b = gated_deltanet_branch(qkv, z, beta_raw, alpha_raw, gdn_conv_w, gdn_a_log, gdn_dt_bias, gdn_norm_w)
    y_c = rwkv7_branch(rw, rwkv_mu, rwkv_w0, rwkv_w_up, rwkv_a0, rwkv_a_up, rwkv_g_up, rwkv_k_k, rwkv_k_a,
                       rwkv_r_k, rwkv_ln_w, rwkv_ln_b)
    gates = jax.nn.sigmoid(gate_raw.astype(jnp.float32)).astype(h.dtype).reshape(B, S, N_BRANCH, D)
    mixed = (gates[:, :, 0] * (y_a @ w_branch[0]) + gates[:, :, 1] * (y_b @ w_branch[1])
             + gates[:, :, 2] * (y_c @ w_branch[2]))
    return mixed @ w_out


def swiglu(h, w_gate, w_up, w_down):
    return (jax.nn.silu(h @ w_gate) * (h @ w_up)) @ w_down


def setup_inputs(seed: int = 0) -> dict:
    key = jax.random.key(seed)
    ks = iter(jax.random.split(key, 48))
    f32 = jnp.float32
    L, D, BW = DEPTH, D_MODEL, BRANCH_W

    def normal(shape, scale):
        return jax.random.normal(next(ks), shape, f32) * scale

    def unif(shape, lo, hi):
        return jax.random.uniform(next(ks), shape, f32, lo, hi)

    def gain(shape):
        return 1.0 + normal(shape, 0.02)

    u = unif((L, BW), 0.9, 0.999)
    a_base = u ** (1.0 / LRU_C)
    lru_lambda = jnp.log(a_base) - jnp.log1p(-a_base)
    dt = jnp.exp(unif((L, GDN_HEADS), float(np.log(1e-3)), float(np.log(1e-1))))
    gdn_dt_bias = dt + jnp.log(-jnp.expm1(-dt))
    return {
        'x': jax.random.normal(next(ks), (BATCH, SEQ, D), f32),
        'norm_mix_pre': gain((L, D)),
        'norm_mix_post': gain((L, D)),
        'norm_ffn_pre': gain((L, D)),
        'norm_ffn_post': gain((L, D)),
        'w_in': normal((L, D, N_IN), D ** -0.5),
        'lru_conv_w': normal((L, CONV_W, BW), CONV_W ** -0.5),
        'lru_conv_b': normal((L, BW), 0.02),
        'lru_w_a': normal((L, LRU_BLOCKS, LRU_BLOCK, LRU_BLOCK), LRU_BLOCK ** -0.5),
        'lru_b_a': normal((L, BW), 0.02),
        'lru_w_x': normal((L, LRU_BLOCKS, LRU_BLOCK, LRU_BLOCK), LRU_BLOCK ** -0.5),
        'lru_b_x': normal((L, BW), 0.02),
        'lru_lambda': lru_lambda,
        'gdn_conv_w': normal((L, CONV_W, 3 * BW), CONV_W ** -0.5),
        'gdn_a_log': jnp.log(unif((L, GDN_HEADS), 1.0, 16.0)),
        'gdn_dt_bias': gdn_dt_bias,
        'gdn_norm_w': gain((L, GDN_HEAD_DIM)),
        'rwkv_mu': unif((L, RWKV_COLS), 0.0, 1.0),
        'rwkv_w0': unif((L, BW), -6.0, -1.0),
        'rwkv_w_up': normal((L, RWKV_W_RANK, BW), 0.05),
        'rwkv_a0': normal((L, BW), 0.1),
        'rwkv_a_up': normal((L, RWKV_A_RANK, BW), 0.05),
        'rwkv_g_up': normal((L, RWKV_G_RANK, BW), RWKV_G_RANK ** -0.5),
        'rwkv_k_k': 0.85 + normal((L, BW), 0.02),
        'rwkv_k_a': gain((L, BW)),
        'rwkv_r_k': normal((L, RWKV_HEADS, RWKV_HEAD), 0.1),
        'rwkv_ln_w': gain((L, BW)),
        'rwkv_ln_b': normal((L, BW), 0.02),
        'w_branch': normal((L, N_BRANCH, BW, D), BW ** -0.5),
        'w_out': normal((L, D, D), D ** -0.5),
        'ffn_w_gate': normal((L, D, D_FF), D ** -0.5),
        'ffn_w_up': normal((L, D, D_FF), D ** -0.5),
        'ffn_w_down': normal((L, D_FF, D), D_FF ** -0.5),
    }


def reference(x, norm_mix_pre, norm_mix_post, norm_ffn_pre, norm_ffn_post, w_in,
              lru_conv_w, lru_conv_b, lru_w_a, lru_b_a, lru_w_x, lru_b_x, lru_lambda,
              gdn_conv_w, gdn_a_log, gdn_dt_bias, gdn_norm_w,
              rwkv_mu, rwkv_w0, rwkv_w_up, rwkv_a0, rwkv_a_up, rwkv_g_up, rwkv_k_k, rwkv_k_a, rwkv_r_k,
              rwkv_ln_w, rwkv_ln_b, w_branch, w_out, ffn_w_gate, ffn_w_up, ffn_w_down):
    for l in range(DEPTH):
        h = rms_norm(x, norm_mix_pre[l])
        h = hybrid_mixer(h, w_in[l], lru_conv_w[l], lru_conv_b[l], lru_w_a[l], lru_b_a[l], lru_w_x[l], lru_b_x[l],
                         lru_lambda[l], gdn_conv_w[l], gdn_a_log[l], gdn_dt_bias[l], gdn_norm_w[l],
                         rwkv_mu[l], rwkv_w0[l], rwkv_w_up[l], rwkv_a0[l], rwkv_a_up[l], rwkv_g_up[l], rwkv_k_k[l],
                         rwkv_k_a[l], rwkv_r_k[l], rwkv_ln_w[l], rwkv_ln_b[l], w_branch[l], w_out[l])
        x = x + rms_norm(h, norm_mix_post[l])
        h = rms_norm(x, norm_ffn_pre[l])
        h = swiglu(h, ffn_w_gate[l], ffn_w_up[l], ffn_w_down[l])
        x = x + rms_norm(h, norm_ffn_post[l])
    return x
```

```python
import functools

import jax
import jax.numpy as jnp
import numpy as np
from jax import lax
from jax.experimental import pallas as pl
from jax.experimental.pallas import tpu as pltpu

F32 = jnp.float32
BF16 = jnp.bfloat16

EPS = 1e-6
LRU_C = 8.0
LRU_BLOCK = 64
GDN_HEADS = 8
GDN_HEAD_DIM = 128
RWKV_HEAD = 64
RWKV_W_RANK = 64
RWKV_A_RANK = 64
RWKV_G_RANK = 160
RWKV_GN_EPS = 64e-5
N_BRANCH = 3
CONV_W = 4

CHUNK = 64
LANES = 128
SUBLANES = 8
VMEM_LIMIT = 56 * 1024 * 1024

SMALL_W = 512


def _layout(d_model, branch_w):
    off = {}
    c = 0
    for name, width in (("lru_x", branch_w), ("lru_g", branch_w), ("gdn_q", branch_w), ("gdn_k", branch_w),
                        ("gdn_v", branch_w), ("gdn_z", branch_w), ("rw_r", branch_w), ("rw_k", branch_w),
                        ("rw_v", branch_w), ("gates", N_BRANCH * d_model), ("small", SMALL_W)):
        off[name] = c
        c += width
    off["total"] = c
    return off


SM_WLO = 0
SM_ALO = SM_WLO + RWKV_W_RANK
SM_GLO = SM_ALO + RWKV_A_RANK
SM_BETA = SM_GLO + RWKV_G_RANK
SM_ALPHA = SM_BETA + GDN_HEADS
SM_USED = SM_ALPHA + GDN_HEADS
SM_GLO_PAD = 256


def _params(*sem):
    return pltpu.CompilerParams(dimension_semantics=sem, vmem_limit_bytes=VMEM_LIMIT)


def _dot(a, b):
    return jnp.dot(a, b, preferred_element_type=F32)


def _dot_nt(a, b):
    return lax.dot_general(a, b, (((1,), (1,)), ((), ())), preferred_element_type=F32)


def _dot_tn(a, b):
    return lax.dot_general(a, b, (((0,), (0,)), ((), ())), preferred_element_type=F32)


def _softplus(x):
    return jnp.maximum(x, 0.0) + jnp.log1p(jnp.exp(-jnp.abs(x)))


def _sigmoid(x):
    return 1.0 / (1.0 + jnp.exp(-x))


def _silu(x):
    return x * _sigmoid(x)


def _gelu_tanh(x):
    c = np.float32(np.sqrt(2.0 / np.pi))
    return 0.5 * x * (1.0 + jnp.tanh(c * (x + 0.044715 * (x * x * x))))


def _tri_masks(n):
    row = lax.broadcasted_iota(jnp.int32, (n, n), 0)
    col = lax.broadcasted_iota(jnp.int32, (n, n), 1)
    return row >= col, row > col, row == col


def _neumann_inverse_apply(nil, rhs_list):
    p = nil
    outs = list(rhs_list)
    steps = int(np.log2(CHUNK))
    for it in range(steps):
        outs = [o + _dot(p, o) for o in outs]
        if it + 1 < steps:
            p = _dot(p, p)
    return outs


def _seg_cumsum(x, seg):
    row = lax.broadcasted_iota(jnp.int32, x.shape, 0)
    pos = row & (seg - 1)
    k = 1
    while k < seg:
        x = jnp.where(pos >= k, x + pltpu.roll(x, k, axis=0), x)
        k *= 2
    return x


def _shift_rows(x, prev_ref, k):
    rolled = pltpu.roll(x, k, axis=0)
    prev = pltpu.roll(prev_ref[...], k, axis=0)
    t = x.shape[0]
    prev_full = jnp.concatenate([prev, jnp.zeros((t - SUBLANES, x.shape[1]), x.dtype)], axis=0)
    row = lax.broadcasted_iota(jnp.int32, x.shape, 0)
    return jnp.where(row < k, prev_full, rolled)


def _norm_matmul_kernel(x_ref, nw_ref, w_ref, o_ref, h_ref):
    @pl.when(pl.program_id(1) == 0)
    def _():
        x = x_ref[...]
        ms = jnp.mean(x * x, axis=-1, keepdims=True)
        h_ref[...] = (x * lax.rsqrt(ms + EPS) * nw_ref[...]).astype(BF16)

    o_ref[...] = _dot(h_ref[...], w_ref[...])


def _norm_matmul(x, nw, w, tm, tn):
    s, d = x.shape
    n = w.shape[1]
    return pl.pallas_call(
        _norm_matmul_kernel,
        grid=(s // tm, n // tn),
        in_specs=[pl.BlockSpec((tm, d), lambda i, j: (i, 0)),
                  pl.BlockSpec((1, d), lambda i, j: (0, 0)),
                  pl.BlockSpec((d, tn), lambda i, j: (0, j))],
        out_specs=pl.BlockSpec((tm, tn), lambda i, j: (i, j)),
        out_shape=jax.ShapeDtypeStruct((s, n), F32),
        scratch_shapes=[pltpu.VMEM((tm, d), BF16)],
        compiler_params=_params("arbitrary", "arbitrary"),
    )(x, nw, w)


def _lru_kernel(x_ref, g_ref, cw_ref, cb_ref, wax_ref, ba_ref, bx_ref, lam_ref, o_ref, tail_ref, hc_ref):
    t_blk = x_ref.shape[0]
    width = x_ref.shape[1]

    @pl.when(pl.program_id(0) == 0)
    def _():
        tail_ref[...] = jnp.zeros_like(tail_ref)
        hc_ref[...] = jnp.zeros_like(hc_ref)

    x = x_ref[...]
    cw = cw_ref[...]
    xc = x * cw[3:4] + cb_ref[...]
    for k in range(1, CONV_W):
        xc = xc + _shift_rows(x, tail_ref, k) * cw[CONV_W - 1 - k:CONV_W - k]
    tail_ref[...] = x[t_blk - SUBLANES:, :]

    nb = wax_ref.shape[0]
    gw = width // nb
    r_parts, i_parts = [], []
    for b in range(nb):
        ri = _dot(xc[:, gw * b:gw * (b + 1)], wax_ref[b])
        r_parts.append(ri[:, :gw])
        i_parts.append(ri[:, gw:])
    r = _sigmoid(jnp.concatenate(r_parts, axis=1) + ba_ref[...])
    i = _sigmoid(jnp.concatenate(i_parts, axis=1) + bx_ref[...])
    log_a = (-LRU_C) * r * _softplus(-lam_ref[...])
    a = jnp.exp(log_a)
    u = jnp.sqrt(1.0 - jnp.exp(2.0 * log_a)) * (i * xc)

    row = lax.broadcasted_iota(jnp.int32, a.shape, 0)
    k = 1
    while k < t_blk:
        m = row >= k
        u = jnp.where(m, a * pltpu.roll(u, k, axis=0) + u, u)
        a = jnp.where(m, a * pltpu.roll(a, k, axis=0), a)
        k *= 2
    h = u + a * hc_ref[0:1, :]
    hc_ref[...] = jnp.broadcast_to(h[t_blk - 1:t_blk, :], hc_ref.shape)
    o_ref[...] = _gelu_tanh(g_ref[...]) * h


def _lru_branch(p, off, cw, cb, wax, ba, bx, lam, t_blk):
    s = p.shape[0]
    width = cw.shape[1]
    xb = off["lru_x"] // width
    gb = off["lru_g"] // width
    vec = pl.BlockSpec((1, width), lambda t: (0, 0))
    return pl.pallas_call(
        _lru_kernel,
        grid=(s // t_blk,),
        in_specs=[pl.BlockSpec((t_blk, width), lambda t: (t, xb)),
                  pl.BlockSpec((t_blk, width), lambda t: (t, gb)),
                  pl.BlockSpec((CONV_W, width), lambda t: (0, 0)),
                  vec,
                  pl.BlockSpec(wax.shape, lambda t: (0, 0, 0)),
                  vec, vec, vec],
        out_specs=pl.BlockSpec((t_blk, width), lambda t: (t, 0)),
        out_shape=jax.ShapeDtypeStruct((s, width), F32),
        scratch_shapes=[pltpu.VMEM((SUBLANES, width), F32), pltpu.VMEM((SUBLANES, width), F32)],
        compiler_params=_params("arbitrary"),
    )(p, p, cw, cb, wax, ba, bx, lam)


def _gdn_kernel(q_ref, k_ref, v_ref, z_ref, sm_ref, cwq_ref, cwk_ref, cwv_ref, alog_ref, dtb_ref, nw_ref,
                o_ref, tq_ref, tk_ref, tv_ref, st_ref):
    t_blk, hd = q_ref.shape
    head = pl.program_id(0)

    @pl.when(pl.program_id(1) == 0)
    def _():
        tq_ref[...] = jnp.zeros_like(tq_ref)
        tk_ref[...] = jnp.zeros_like(tk_ref)
        tv_ref[...] = jnp.zeros_like(tv_ref)
        st_ref[...] = jnp.zeros_like(st_ref)

    def conv_silu(x_ref, tail_ref, cw_ref):
        x = x_ref[...]
        cw = cw_ref[...]
        y = x * cw[3:4]
        for k in range(1, CONV_W):
            y = y + _shift_rows(x, tail_ref, k) * cw[CONV_W - 1 - k:CONV_W - k]
        tail_ref[...] = x[t_blk - SUBLANES:, :]
        return _silu(y)

    q = conv_silu(q_ref, tq_ref, cwq_ref)
    k = conv_silu(k_ref, tk_ref, cwk_ref)
    v = conv_silu(v_ref, tv_ref, cwv_ref)
    q = q * lax.rsqrt(jnp.sum(q * q, axis=-1, keepdims=True) + EPS) * (hd ** -0.5)
    k = k * lax.rsqrt(jnp.sum(k * k, axis=-1, keepdims=True) + EPS)

    sm = sm_ref[...]
    lane = lax.broadcasted_iota(jnp.int32, (1, sm.shape[1]), 1)
    beta_raw = jnp.sum(jnp.where(lane == SM_BETA + head, sm, 0.0), axis=1, keepdims=True)
    alpha_raw = jnp.sum(jnp.where(lane == SM_ALPHA + head, sm, 0.0), axis=1, keepdims=True)
    lane_h = lax.broadcasted_iota(jnp.int32, alog_ref.shape, 1)
    a_log = jnp.sum(jnp.where(lane_h == head, alog_ref[...], 0.0), axis=1, keepdims=True)
    dt_b = jnp.sum(jnp.where(lane_h == head, dtb_ref[...], 0.0), axis=1, keepdims=True)
    beta = jnp.broadcast_to(_sigmoid(beta_raw), (t_blk, hd))
    g = jnp.broadcast_to(-jnp.exp(a_log) * _softplus(alpha_raw + dt_b), (t_blk, hd))
    gc = _seg_cumsum(g, CHUNK)
    egc = jnp.exp(gc)

    causal, strict, eye = _tri_masks(CHUNK)
    state = st_ref[...]
    outs = []
    for c in range(t_blk // CHUNK):
        sl = slice(c * CHUNK, (c + 1) * CHUNK)
        qc, kc, vc, bc, gcc, egcc = q[sl], k[sl], v[sl], beta[sl], gc[sl], egc[sl]
        g_col = gcc[:, :CHUNK]
        g_row = jnp.sum(jnp.where(eye, g_col, 0.0), axis=0, keepdims=True)
        decay = jnp.where(causal, jnp.exp(jnp.where(causal, g_col - g_row, 0.0)), 0.0)
        kk = _dot_nt(kc, kc)
        nil = jnp.where(strict, -(bc[:, :CHUNK] * kk * decay), 0.0)
        u_v, w_k = _neumann_inverse_apply(nil, [bc * vc, bc * egcc * kc])
        attn = jnp.where(causal, _dot_nt(qc, kc) * decay, 0.0)
        g_last = gcc[CHUNK - 1:CHUNK, :]
        k_dec = kc * jnp.exp(g_last - gcc)
        q_dec = qc * egcc

        u = u_v - _dot(w_k, state)
        outs.append(_dot(q_dec, state) + _dot(attn, u))
        state = state * jnp.exp(g_last) + _dot_tn(k_dec, u)
    st_ref[...] = state

    o = jnp.concatenate(outs, axis=0)
    o = o * lax.rsqrt(jnp.mean(o * o, axis=-1, keepdims=True) + EPS) * nw_ref[...]
    o_ref[...] = o * _silu(z_ref[...])


def _gdn_branch(p, off, conv_w, a_log, dt_bias, norm_w, t_blk):
    s = p.shape[0]
    hd = GDN_HEAD_DIM
    nh = GDN_HEADS
    qb, kb, vb, zb = (off[n] // hd for n in ("gdn_q", "gdn_k", "gdn_v", "gdn_z"))
    smb = off["small"] // SMALL_W
    blk = lambda base: pl.BlockSpec((t_blk, hd), lambda h, t: (t, base + h))
    cwb = lambda base: pl.BlockSpec((CONV_W, hd), lambda h, t: (0, base + h))
    return pl.pallas_call(
        _gdn_kernel,
        grid=(nh, s // t_blk),
        in_specs=[blk(qb), blk(kb), blk(vb), blk(zb),
                  pl.BlockSpec((t_blk, SMALL_W), lambda h, t: (t, smb)),
                  cwb(0), cwb(nh), cwb(2 * nh),
                  pl.BlockSpec((1, nh), lambda h, t: (0, 0)),
                  pl.BlockSpec((1, nh), lambda h, t: (0, 0)),
                  pl.BlockSpec((1, hd), lambda h, t: (0, 0))],
        out_specs=pl.BlockSpec((t_blk, hd), lambda h, t: (t, h)),
        out_shape=jax.ShapeDtypeStruct((s, nh * hd), F32),
        scratch_shapes=[pltpu.VMEM((SUBLANES, hd), F32)] * 3 + [pltpu.VMEM((hd, hd), F32)],
        compiler_params=_params("arbitrary", "arbitrary"),
    )(p, p, p, p, p, conv_w, conv_w, conv_w, a_log, dt_bias, norm_w)


def _rwkv_kernel(r_ref, k_ref, v_ref, sm_ref, mur_ref, muk_ref, muv_ref, mus_ref, w0_ref, a0_ref, kk_ref, ka_ref,
                 rk_ref, lnw_ref, lnb_ref, wup_ref, aup_ref, gup_ref,
                 o_ref, tr_ref, tk_ref, tv_ref, ts_ref, st_ref):
    t_blk, pw = r_ref.shape
    hd = RWKV_HEAD
    nh = pw // hd

    @pl.when(pl.program_id(1) == 0)
    def _():
        tr_ref[...] = jnp.zeros_like(tr_ref)
        tk_ref[...] = jnp.zeros_like(tk_ref)
        tv_ref[...] = jnp.zeros_like(tv_ref)
        ts_ref[...] = jnp.zeros_like(ts_ref)
        st_ref[...] = jnp.zeros_like(st_ref)

    def token_shift(x_ref, tail_ref, mu_ref):
        x = x_ref[...]
        prev = _shift_rows(x, tail_ref, 1)
        tail_ref[...] = x[t_blk - SUBLANES:, :]
        return x + (prev - x) * mu_ref[...]

    r = token_shift(r_ref, tr_ref, mur_ref)
    k = token_shift(k_ref, tk_ref, muk_ref)
    v = token_shift(v_ref, tv_ref, muv_ref)
    sm = token_shift(sm_ref, ts_ref, mus_ref)

    lo = sm[:, SM_WLO:SM_WLO + LANES]
    w = -_softplus(-(w0_ref[...] + _dot(jnp.tanh(lo), wup_ref[...]))) - 0.5
    a = _sigmoid(a0_ref[...] + _dot(lo, aup_ref[...]))
    g = _dot(_sigmoid(sm[:, SM_GLO:SM_GLO + SM_GLO_PAD]), gup_ref[...])
    log_dec = -jnp.exp(w)

    lane = lax.broadcasted_iota(jnp.int32, (t_blk, pw), 1)

    def head_sum(x):
        tot = jnp.zeros_like(x)
        for h in range(nh):
            sh = jnp.sum(x[:, h * hd:(h + 1) * hd], axis=1, keepdims=True)
            tot = jnp.where((lane >= h * hd) & (lane < (h + 1) * hd), sh, tot)
        return tot

    kraw = k * kk_ref[...]
    kk = kraw * lax.rsqrt(head_sum(kraw * kraw) + EPS)
    k2 = k * (1.0 + (a - 1.0) * ka_ref[...])

    gcs = _seg_cumsum(log_dec, CHUNK)
    e_inc = jnp.exp(gcs)
    e_exc = jnp.exp(gcs - log_dec)
    e_neg = jnp.exp(-gcs)
    xa = -kk * e_exc
    xr = r * e_inc
    yb = kk * a * e_neg
    yk = k2 * e_neg

    causal, strict, eye = _tri_masks(CHUNK)
    eye_f = jnp.where(eye, 1.0, 0.0).astype(F32)
    head_outs = []
    for h in range(nh):
        hs = slice(h * hd, (h + 1) * hd)
        state = st_ref[h]
        outs = []
        for c in range(t_blk // CHUNK):
            sl = slice(c * CHUNK, (c + 1) * CHUNK)
            xa_c, xr_c, yb_c, yk_c, v_c = xa[sl, hs], xr[sl, hs], yb[sl, hs], yk[sl, hs], v[sl, hs]
            m = _dot_nt(jnp.concatenate([xa_c, xr_c], axis=0), jnp.concatenate([yb_c, yk_c], axis=0))
            a_ab = jnp.where(strict, m[:CHUNK, :CHUNK], 0.0)
            a_ak = jnp.where(strict, m[:CHUNK, CHUNK:], 0.0)
            a_rb = jnp.where(causal, m[CHUNK:, :CHUNK], 0.0)
            a_rk = jnp.where(causal, m[CHUNK:, CHUNK:], 0.0)
            w_mat, u0 = _neumann_inverse_apply(a_ab, [xa_c, _dot(a_ak, v_c)])
            d_last = e_inc[(c + 1) * CHUNK - 1:(c + 1) * CHUNK, hs]

            u = u0 + _dot_nt(w_mat, state)
            outs.append(_dot_nt(xr_c, state) + _dot(a_rb, u) + _dot(a_rk, v_c))
            state = (state + _dot_tn(u, yb_c) + _dot_tn(v_c, yk_c)) * d_last
        st_ref[h] = state
        head_outs.append(jnp.concatenate(outs, axis=0))
    out = jnp.concatenate(head_outs, axis=1)

    inv_n = 1.0 / hd
    mean = head_sum(out) * inv_n
    cen = out - mean
    var = head_sum(cen * cen) * inv_n
    gn = cen * lax.rsqrt(var + RWKV_GN_EPS) * lnw_ref[...] + lnb_ref[...]
    bonus = head_sum(r * k2 * rk_ref[...]) * v
    o_ref[...] = (gn + bonus) * g


def _rwkv_branch(p, off, mu_r, mu_k, mu_v, mu_s, w0, a0, k_k, k_a, r_k, ln_w, ln_b, w_up, a_up, g_up, t_blk):
    s = p.shape[0]
    pw = LANES
    width = w0.shape[1]
    npair = width // pw
    rb, kb, vb = (off[n] // pw for n in ("rw_r", "rw_k", "rw_v"))
    smb = off["small"] // SMALL_W
    blk = lambda base: pl.BlockSpec((t_blk, pw), lambda h, t: (t, base + h))
    vec = pl.BlockSpec((1, pw), lambda h, t: (0, h))
    return pl.pallas_call(
        _rwkv_kernel,
        grid=(npair, s // t_blk),
        in_specs=[blk(rb), blk(kb), blk(vb),
                  pl.BlockSpec((t_blk, SMALL_W), lambda h, t: (t, smb)),
                  vec, vec, vec,
                  pl.BlockSpec((1, SMALL_W), lambda h, t: (0, 0)),
                  vec, vec, vec, vec, vec, vec, vec,
                  pl.BlockSpec((LANES, pw), lambda h, t: (0, h)),
                  pl.BlockSpec((LANES, pw), lambda h, t: (0, h)),
                  pl.BlockSpec((SM_GLO_PAD, pw), lambda h, t: (0, h))],
        out_specs=pl.BlockSpec((t_blk, pw), lambda h, t: (t, h)),
        out_shape=jax.ShapeDtypeStruct((s, width), F32),
        scratch_shapes=[pltpu.VMEM((SUBLANES, pw), F32)] * 3 + [pltpu.VMEM((SUBLANES, SMALL_W), F32),
                                                               pltpu.VMEM((pw // RWKV_HEAD, RWKV_HEAD, RWKV_HEAD), F32)],
        compiler_params=_params("arbitrary", "arbitrary"),
    )(p, p, p, p, mu_r, mu_k, mu_v, mu_s, w0, a0, k_k, k_a, r_k, ln_w, ln_b, w_up, a_up, g_up)


def _merge_kernel(ya_ref, yb_ref, yc_ref, ga_ref, gb_ref, gc_ref, w_ref, o_ref):
    acc = None
    for b, (y_ref, g_ref) in enumerate(((ya_ref, ga_ref), (yb_ref, gb_ref), (yc_ref, gc_ref))):
        term = _sigmoid(g_ref[...]) * _dot(y_ref[...].astype(BF16), w_ref[b])
        acc = term if acc is None else acc + term
    o_ref[...] = acc.astype(o_ref.dtype)


def _merge(ya, yb, yc, p, off, w_branch, tm, tn):
    s, bw = ya.shape
    d = w_branch.shape[2]
    gbase = off["gates"] // tn
    nj = d // tn
    yspec = pl.BlockSpec((tm, bw), lambda i, j: (i, 0))
    gspec = lambda b: pl.BlockSpec((tm, tn), lambda i, j: (i, gbase + b * nj + j))
    return pl.pallas_call(
        _merge_kernel,
        grid=(s // tm, nj),
        in_specs=[yspec, yspec, yspec, gspec(0), gspec(1), gspec(2),
                  pl.BlockSpec((N_BRANCH, bw, tn), lambda i, j: (0, 0, j))],
        out_specs=pl.BlockSpec((tm, tn), lambda i, j: (i, j)),
        out_shape=jax.ShapeDtypeStruct((s, d), BF16),
        compiler_params=_params("arbitrary", "arbitrary"),
    )(ya, yb, yc, p, p, p, w_branch)


def _out_proj_kernel(m_ref, w_ref, nw_ref, x_ref, o_ref):
    h = _dot(m_ref[...], w_ref[...])
    ms = jnp.mean(h * h, axis=-1, keepdims=True)
    o_ref[...] = x_ref[...] + h * lax.rsqrt(ms + EPS) * nw_ref[...]


def _out_proj(mixed, w_out, nw, x, tm):
    s, d = x.shape
    return pl.pallas_call(
        _out_proj_kernel,
        grid=(s // tm,),
        in_specs=[pl.BlockSpec((tm, d), lambda i: (i, 0)),
                  pl.BlockSpec((d, d), lambda i: (0, 0)),
                  pl.BlockSpec((1, d), lambda i: (0, 0)),
                  pl.BlockSpec((tm, d), lambda i: (i, 0))],
        out_specs=pl.BlockSpec((tm, d), lambda i: (i, 0)),
        out_shape=jax.ShapeDtypeStruct((s, d), F32),
        compiler_params=_params("arbitrary"),
    )(mixed, w_out, nw, x)


def _ffn_kernel(x_ref, npre_ref, npost_ref, wg_ref, wu_ref, wd_ref, o_ref, h_ref, acc_ref):
    kk = pl.program_id(1)

    @pl.when(kk == 0)
    def _():
        x = x_ref[...]
        ms = jnp.mean(x * x, axis=-1, keepdims=True)
        h_ref[...] = (x * lax.rsqrt(ms + EPS) * npre_ref[...]).astype(BF16)
        acc_ref[...] = jnp.zeros_like(acc_ref)

    h = h_ref[...]
    act = (_silu(_dot(h, wg_ref[...])) * _dot(h, wu_ref[...])).astype(BF16)
    acc_ref[...] += _dot(act, wd_ref[...])

    @pl.when(kk == pl.num_programs(1) - 1)
    def _():
        y = acc_ref[...]
        ms = jnp.mean(y * y, axis=-1, keepdims=True)
        o_ref[...] = x_ref[...] + y * lax.rsqrt(ms + EPS) * npost_ref[...]


def _ffn(x, npre, npost, wg, wu, wd, tm, tf):
    s, d = x.shape
    f = wg.shape[1]
    return pl.pallas_call(
        _ffn_kernel,
        grid=(s // tm, f // tf),
        in_specs=[pl.BlockSpec((tm, d), lambda i, k: (i, 0)),
                  pl.BlockSpec((1, d), lambda i, k: (0, 0)),
                  pl.BlockSpec((1, d), lambda i, k: (0, 0)),
                  pl.BlockSpec((d, tf), lambda i, k: (0, k)),
                  pl.BlockSpec((d, tf), lambda i, k: (0, k)),
                  pl.BlockSpec((tf, d), lambda i, k: (k, 0))],
        out_specs=pl.BlockSpec((tm, d), lambda i, k: (i, 0)),
        out_shape=jax.ShapeDtypeStruct((s, d), F32),
        scratch_shapes=[pltpu.VMEM((tm, d), BF16), pltpu.VMEM((tm, d), F32)],
        compiler_params=_params("arbitrary", "arbitrary"),
    )(x, npre, npost, wg, wu, wd)


def _split_w_in(w_in, d_model, bw):
    sizes = (bw, bw, 3 * bw, bw, GDN_HEADS, GDN_HEADS,
             3 * bw + RWKV_W_RANK + RWKV_A_RANK + RWKV_G_RANK, N_BRANCH * d_model)
    idx = [int(i) for i in np.cumsum(sizes)[:-1]]
    return jnp.split(w_in, idx, axis=-1)


def _permute_w_in(w_in, d_model, bw, total):
    lru_x, lru_g, qkv, z, beta, alpha, rw, gates = _split_w_in(w_in, d_model, bw)
    rkv, lo = rw[:, :3 * bw], rw[:, 3 * bw:]
    small = jnp.concatenate([lo, beta, alpha], axis=1)
    used = 9 * bw + N_BRANCH * d_model + small.shape[1]
    pad = jnp.zeros((w_in.shape[0], total - used), w_in.dtype)
    return jnp.concatenate([lru_x, lru_g, qkv, z, rkv, gates, small, pad], axis=1).astype(BF16)


def _block_diag_gates(w_a, w_x, group):
    nblk, bs, _ = w_a.shape
    ng = nblk // group
    gw = group * bs
    out = jnp.zeros((ng, gw, 2 * gw), F32)
    for b in range(nblk):
        gi, bi = divmod(b, group)
        out = out.at[gi, bi * bs:(bi + 1) * bs, bi * bs:(bi + 1) * bs].set(w_a[b])
        out = out.at[gi, bi * bs:(bi + 1) * bs, gw + bi * bs:gw + (bi + 1) * bs].set(w_x[b])
    return out


def _pad_rows(w, rows, at):
    out = jnp.zeros((rows, w.shape[1]), w.dtype)
    return out.at[at:at + w.shape[0]].set(w)


def _tiles(s):
    return dict(tm_in=min(s, 1024), tn_in=512, t_lru=min(s, 256), t_gdn=min(s, 512), t_rwkv=min(s, 256),
                tm_merge=min(s, 512), tn_merge=1024, tm_out=min(s, 512), tm_ffn=min(s, 512), tf_ffn=512)


def _layer(x, lp, tiles):
    s, d = x.shape
    bw = lp["lru_conv_w"].shape[1]
    off = _layout(d, bw)
    row = lambda v: v.reshape(1, -1)

    p = _norm_matmul(x, row(lp["norm_mix_pre"]), lp["w_in_perm"], tiles["tm_in"], tiles["tn_in"])

    y_a = _lru_branch(p, off, lp["lru_conv_w"], row(lp["lru_conv_b"]), lp["lru_wax"], row(lp["lru_b_a"]),
                      row(lp["lru_b_x"]), row(lp["lru_lambda"]), tiles["t_lru"])
    y_b = _gdn_branch(p, off, lp["gdn_conv_w"], row(lp["gdn_a_log"]), row(lp["gdn_dt_bias"]),
                      row(lp["gdn_norm_w"]), tiles["t_gdn"])
    mu = lp["rwkv_mu"]
    mu_s = jnp.zeros((1, SMALL_W), F32).at[0, :SM_BETA].set(mu[3 * bw:])
    y_c = _rwkv_branch(p, off, row(mu[:bw]), row(mu[bw:2 * bw]), row(mu[2 * bw:3 * bw]), mu_s,
                       row(lp["rwkv_w0"]), row(lp["rwkv_a0"]), row(lp["rwkv_k_k"]), row(lp["rwkv_k_a"]),
                       row(lp["rwkv_r_k"]), row(lp["rwkv_ln_w"]), row(lp["rwkv_ln_b"]),
                       _pad_rows(lp["rwkv_w_up"], LANES, SM_WLO), _pad_rows(lp["rwkv_a_up"], LANES, SM_ALO),
                       _pad_rows(lp["rwkv_g_up"], SM_GLO_PAD, 0), tiles["t_rwkv"])

    mixed = _merge(y_a, y_b, y_c, p, off, lp["w_branch"], tiles["tm_merge"], tiles["tn_merge"])
    x = _out_proj(mixed, lp["w_out"], row(lp["norm_mix_post"]), x, tiles["tm_out"])
    x = _ffn(x, row(lp["norm_ffn_pre"]), row(lp["norm_ffn_post"]), lp["ffn_w_gate"], lp["ffn_w_up"],
             lp["ffn_w_down"], tiles["tm_ffn"], tiles["tf_ffn"])
    return x


def kernel(x, norm_mix_pre, norm_mix_post, norm_ffn_pre, norm_ffn_post, w_in, lru_conv_w, lru_conv_b, lru_w_a, lru_b_a, lru_w_x, lru_b_x, lru_lambda, gdn_conv_w, gdn_a_log, gdn_dt_bias, gdn_norm_w, rwkv_mu, rwkv_w0, rwkv_w_up, rwkv_a0, rwkv_a_up, rwkv_g_up, rwkv_k_k, rwkv_k_a, rwkv_r_k, rwkv_ln_w, rwkv_ln_b, w_branch, w_out, ffn_w_gate, ffn_w_up, ffn_w_down):
    batch, s, d = x.shape
    depth = w_in.shape[0]
    bw = lru_conv_w.shape[2]
    total = _layout(d, bw)["total"]
    tiles = _tiles(s)
    outs = []
    for b in range(batch):
        xb = x[b]
        for l in range(depth):
            lp = dict(
                norm_mix_pre=norm_mix_pre[l], norm_mix_post=norm_mix_post[l],
                norm_ffn_pre=norm_ffn_pre[l], norm_ffn_post=norm_ffn_post[l],
                w_in_perm=_permute_w_in(w_in[l], d, bw, total),
                lru_conv_w=lru_conv_w[l], lru_conv_b=lru_conv_b[l],
                lru_wax=_block_diag_gates(lru_w_a[l], lru_w_x[l], 2 * LANES // LRU_BLOCK),
                lru_b_a=lru_b_a[l], lru_b_x=lru_b_x[l], lru_lambda=lru_lambda[l],
                gdn_conv_w=gdn_conv_w[l], gdn_a_log=gdn_a_log[l], gdn_dt_bias=gdn_dt_bias[l],
                gdn_norm_w=gdn_norm_w[l],
                rwkv_mu=rwkv_mu[l], rwkv_w0=rwkv_w0[l], rwkv_w_up=rwkv_w_up[l], rwkv_a0=rwkv_a0[l],
                rwkv_a_up=rwkv_a_up[l], rwkv_g_up=rwkv_g_up[l], rwkv_k_k=rwkv_k_k[l], rwkv_k_a=rwkv_k_a[l],
                rwkv_r_k=rwkv_r_k[l], rwkv_ln_w=rwkv_ln_w[l], rwkv_ln_b=rwkv_ln_b[l],
                w_branch=w_branch[l].astype(BF16), w_out=w_out[l].astype(BF16),
                ffn_w_gate=ffn_w_gate[l].astype(BF16), ffn_w_up=ffn_w_up[l].astype(BF16),
                ffn_w_down=ffn_w_down[l].astype(BF16))
            xb = _layer(xb, lp, tiles)
        outs.append(xb)
    return jnp.stack(outs, axis=0)
```

```python
import jax
import jax.numpy as jnp
import numpy as np
from jax import lax
from jax.experimental import pallas as pl
from jax.experimental.pallas import tpu as pltpu

F32 = jnp.float32
BF16 = jnp.bfloat16

EPS = 1e-6
LRU_C = 8.0
LRU_BLOCK = 64
GDN_HEADS = 8
GDN_HEAD_DIM = 128
RWKV_HEAD = 64
RWKV_W_RANK = 64
RWKV_A_RANK = 64
RWKV_G_RANK = 160
RWKV_GN_EPS = 64e-5
N_BRANCH = 3
CONV_W = 4

CHUNK = 64
LANES = 128
SUBLANES = 8
VMEM_LIMIT = 56 * 1024 * 1024

SMALL_W = 512


def _layout(d_model, branch_w):
    off = {}
    c = 0
    for name, width in (("lru_x", branch_w), ("lru_g", branch_w), ("gdn_q", branch_w), ("gdn_k", branch_w),
                        ("gdn_v", branch_w), ("gdn_z", branch_w), ("rw_r", branch_w), ("rw_k", branch_w),
                        ("rw_v", branch_w), ("gates", N_BRANCH * d_model), ("small", SMALL_W)):
        off[name] = c
        c += width
    off["total"] = c
    return off


SM_WLO = 0
SM_ALO = SM_WLO + RWKV_W_RANK
SM_GLO = SM_ALO + RWKV_A_RANK
SM_BETA = SM_GLO + RWKV_G_RANK
SM_ALPHA = SM_BETA + GDN_HEADS
SM_USED = SM_ALPHA + GDN_HEADS
SM_GLO_PAD = 256


def _params(*sem):
    return pltpu.CompilerParams(dimension_semantics=sem, vmem_limit_bytes=VMEM_LIMIT)


def _dot(a, b):
    return jnp.dot(a, b, preferred_element_type=F32)


def _dot_nt(a, b):
    return lax.dot_general(a, b, (((1,), (1,)), ((), ())), preferred_element_type=F32)


def _dot_tn(a, b):
    return lax.dot_general(a, b, (((0,), (0,)), ((), ())), preferred_element_type=F32)


def _softplus(x):
    return jnp.maximum(x, 0.0) + jnp.log1p(jnp.exp(-jnp.abs(x)))


def _sigmoid(x):
    return 1.0 / (1.0 + jnp.exp(-x))


def _silu(x):
    return x * _sigmoid(x)


def _gelu_tanh(x):
    c = np.float32(np.sqrt(2.0 / np.pi))
    return 0.5 * x * (1.0 + jnp.tanh(c * (x + 0.044715 * (x * x * x))))


def _tri_masks(n):
    row = lax.broadcasted_iota(jnp.int32, (n, n), 0)
    col = lax.broadcasted_iota(jnp.int32, (n, n), 1)
    return row >= col, row > col, row == col


def _neumann_solve(nils, xs):
    steps = int(np.log2(CHUNK))
    for it in range(steps):
        xs = [x + _dot(p, x) for p, x in zip(nils, xs)]
        if it + 1 < steps:
            nils = [_dot(p, p) for p in nils]
    return xs


def _seg_cumsum(x, seg):
    row = lax.broadcasted_iota(jnp.int32, x.shape, 0)
    pos = row & (seg - 1)
    k = 1
    while k < seg:
        x = jnp.where(pos >= k, x + pltpu.roll(x, k, axis=0), x)
        k *= 2
    return x


def _shift_rows(x, prev_ref, k):
    rolled = pltpu.roll(x, k, axis=0)
    prev = pltpu.roll(prev_ref[...], k, axis=0)
    t = x.shape[0]
    prev_full = jnp.concatenate([prev, jnp.zeros((t - SUBLANES, x.shape[1]), x.dtype)], axis=0)
    row = lax.broadcasted_iota(jnp.int32, x.shape, 0)
    return jnp.where(row < k, prev_full, rolled)


def _norm_matmul_kernel(x_ref, nw_ref, w_ref, o_ref, h_ref):
    @pl.when(pl.program_id(1) == 0)
    def _():
        x = x_ref[...]
        ms = jnp.mean(x * x, axis=-1, keepdims=True)
        h_ref[...] = (x * lax.rsqrt(ms + EPS) * nw_ref[...]).astype(BF16)

    o_ref[...] = _dot(h_ref[...], w_ref[...])


def _norm_matmul(x, nw, w, tm, tn):
    s, d = x.shape
    n = w.shape[1]
    return pl.pallas_call(
        _norm_matmul_kernel,
        grid=(s // tm, n // tn),
        in_specs=[pl.BlockSpec((tm, d), lambda i, j: (i, 0)),
                  pl.BlockSpec((1, d), lambda i, j: (0, 0)),
                  pl.BlockSpec((d, tn), lambda i, j: (0, j))],
        out_specs=pl.BlockSpec((tm, tn), lambda i, j: (i, j)),
        out_shape=jax.ShapeDtypeStruct((s, n), F32),
        scratch_shapes=[pltpu.VMEM((tm, d), BF16)],
        compiler_params=_params("arbitrary", "arbitrary"),
    )(x, nw, w)


def _lru_kernel(x_ref, g_ref, cw_ref, cb_ref, wax_ref, ba_ref, bx_ref, lam_ref, o_ref, tail_ref, hc_ref):
    t_blk = x_ref.shape[0]
    width = x_ref.shape[1]

    @pl.when(pl.program_id(0) == 0)
    def _():
        tail_ref[...] = jnp.zeros_like(tail_ref)
        hc_ref[...] = jnp.zeros_like(hc_ref)

    x = x_ref[...]
    cw = cw_ref[...]
    xc = x * cw[3:4] + cb_ref[...]
    for k in range(1, CONV_W):
        xc = xc + _shift_rows(x, tail_ref, k) * cw[CONV_W - 1 - k:CONV_W - k]
    tail_ref[...] = x[t_blk - SUBLANES:, :]

    nb = wax_ref.shape[0]
    gw = width // nb
    r_parts, i_parts = [], []
    for b in range(nb):
        ri = _dot(xc[:, gw * b:gw * (b + 1)], wax_ref[b])
        r_parts.append(ri[:, :gw])
        i_parts.append(ri[:, gw:])
    r = _sigmoid(jnp.concatenate(r_parts, axis=1) + ba_ref[...])
    i = _sigmoid(jnp.concatenate(i_parts, axis=1) + bx_ref[...])
    log_a = (-LRU_C) * r * _softplus(-lam_ref[...])
    a = jnp.exp(log_a)
    u = jnp.sqrt(1.0 - jnp.exp(2.0 * log_a)) * (i * xc)

    row = lax.broadcasted_iota(jnp.int32, a.shape, 0)
    k = 1
    while k < t_blk:
        m = row >= k
        u = jnp.where(m, a * pltpu.roll(u, k, axis=0) + u, u)
        a = jnp.where(m, a * pltpu.roll(a, k, axis=0), a)
        k *= 2
    h = u + a * hc_ref[0:1, :]
    hc_ref[...] = jnp.broadcast_to(h[t_blk - 1:t_blk, :], hc_ref.shape)
    o_ref[...] = _gelu_tanh(g_ref[...]) * h


def _lru_branch(p, off, cw, cb, wax, ba, bx, lam, t_blk):
    s = p.shape[0]
    width = cw.shape[1]
    xb = off["lru_x"] // width
    gb = off["lru_g"] // width
    vec = pl.BlockSpec((1, width), lambda t: (0, 0))
    return pl.pallas_call(
        _lru_kernel,
        grid=(s // t_blk,),
        in_specs=[pl.BlockSpec((t_blk, width), lambda t: (t, xb)),
                  pl.BlockSpec((t_blk, width), lambda t: (t, gb)),
                  pl.BlockSpec((CONV_W, width), lambda t: (0, 0)),
                  vec,
                  pl.BlockSpec(wax.shape, lambda t: (0, 0, 0)),
                  vec, vec, vec],
        out_specs=pl.BlockSpec((t_blk, width), lambda t: (t, 0)),
        out_shape=jax.ShapeDtypeStruct((s, width), F32),
        scratch_shapes=[pltpu.VMEM((SUBLANES, width), F32), pltpu.VMEM((SUBLANES, width), F32)],
        compiler_params=_params("arbitrary"),
    )(p, p, cw, cb, wax, ba, bx, lam)


def _gdn_kernel(q_ref, k_ref, v_ref, z_ref, sm_ref, cwq_ref, cwk_ref, cwv_ref, alog_ref, dtb_ref, nw_ref,
                o_ref, tq_ref, tk_ref, tv_ref, st_ref):
    t_blk = q_ref.shape[0]
    hd = GDN_HEAD_DIM
    nh = q_ref.shape[1] // hd
    n_chunks = t_blk // CHUNK
    head0 = pl.program_id(0) * nh

    @pl.when(pl.program_id(1) == 0)
    def _():
        tq_ref[...] = jnp.zeros_like(tq_ref)
        tk_ref[...] = jnp.zeros_like(tk_ref)
        tv_ref[...] = jnp.zeros_like(tv_ref)
        st_ref[...] = jnp.zeros_like(st_ref)

    def conv_silu(x_ref, tail_ref, cw_ref):
        x = x_ref[...]
        cw = cw_ref[...]
        y = x * cw[3:4]
        for k in range(1, CONV_W):
            y = y + _shift_rows(x, tail_ref, k) * cw[CONV_W - 1 - k:CONV_W - k]
        tail_ref[...] = x[t_blk - SUBLANES:, :]
        return _silu(y)

    q_all = conv_silu(q_ref, tq_ref, cwq_ref)
    k_all = conv_silu(k_ref, tk_ref, cwk_ref)
    v_all = conv_silu(v_ref, tv_ref, cwv_ref)

    sm = sm_ref[...]
    lane = lax.broadcasted_iota(jnp.int32, (1, sm.shape[1]), 1)
    lane_h = lax.broadcasted_iota(jnp.int32, alog_ref.shape, 1)
    causal, strict, eye = _tri_masks(CHUNK)

    qs, ks, vs, betas, gcs, egcs = [], [], [], [], [], []
    for h in range(nh):
        hs = slice(h * hd, (h + 1) * hd)
        q = q_all[:, hs]
        k = k_all[:, hs]
        qs.append(q * lax.rsqrt(jnp.sum(q * q, axis=-1, keepdims=True) + EPS) * (hd ** -0.5))
        ks.append(k * lax.rsqrt(jnp.sum(k * k, axis=-1, keepdims=True) + EPS))
        vs.append(v_all[:, hs])
        beta_raw = jnp.sum(jnp.where(lane == SM_BETA + head0 + h, sm, 0.0), axis=1, keepdims=True)
        alpha_raw = jnp.sum(jnp.where(lane == SM_ALPHA + head0 + h, sm, 0.0), axis=1, keepdims=True)
        a_log = jnp.sum(jnp.where(lane_h == head0 + h, alog_ref[...], 0.0), axis=1, keepdims=True)
        dt_b = jnp.sum(jnp.where(lane_h == head0 + h, dtb_ref[...], 0.0), axis=1, keepdims=True)
        betas.append(jnp.broadcast_to(_sigmoid(beta_raw), (t_blk, hd)))
        g = jnp.broadcast_to(-jnp.exp(a_log) * _softplus(alpha_raw + dt_b), (t_blk, hd))
        gc = _seg_cumsum(g, CHUNK)
        gcs.append(gc)
        egcs.append(jnp.exp(gc))

    items = [(h, c) for h in range(nh) for c in range(n_chunks)]

    def rows(arrs, h, c):
        return arrs[h][c * CHUNK:(c + 1) * CHUNK]

    kq = [_dot_nt(jnp.concatenate([rows(ks, h, c), rows(qs, h, c)], axis=0), rows(ks, h, c)) for h, c in items]
    nil, attn, xs = [], [], []
    for (h, c), m in zip(items, kq):
        gcc, bc, kc = rows(gcs, h, c), rows(betas, h, c), rows(ks, h, c)
        g_col = gcc[:, :CHUNK]
        g_row = jnp.sum(jnp.where(eye, g_col, 0.0), axis=0, keepdims=True)
        decay = jnp.exp(jnp.where(causal, g_col - g_row, 0.0))
        nil.append(jnp.where(strict, -(bc[:, :CHUNK] * m[:CHUNK] * decay), 0.0))
        attn.append(jnp.where(causal, m[CHUNK:] * decay, 0.0))
        xs.append(jnp.concatenate([bc * rows(vs, h, c), bc * rows(egcs, h, c) * kc], axis=1))
    xs = _neumann_solve(nil, xs)
    g_last = [rows(gcs, h, c)[CHUNK - 1:CHUNK, :] for h, c in items]
    tn = [_dot_tn(rows(ks, h, c) * jnp.exp(gl - rows(gcs, h, c)), x) for (h, c), gl, x in zip(items, g_last, xs)]
    ax = [_dot(a_, x) for a_, x in zip(attn, xs)]
    q_eff = [rows(qs, h, c) * rows(egcs, h, c) - a_[:, hd:] for (h, c), a_ in zip(items, ax)]

    states = [st_ref[h] for h in range(nh)]
    outs = [[] for _ in range(nh)]
    for c in range(n_chunks):
        for h in range(nh):
            i = h * n_chunks + c
            outs[h].append(_dot(q_eff[i], states[h]) + ax[i][:, :hd])
        states = [states[h] * jnp.exp(g_last[h * n_chunks + c]) - _dot(tn[h * n_chunks + c][:, hd:], states[h])
                  + tn[h * n_chunks + c][:, :hd] for h in range(nh)]
    for h in range(nh):
        st_ref[h] = states[h]

    z = z_ref[...]
    for h in range(nh):
        o = jnp.concatenate(outs[h], axis=0)
        o = o * lax.rsqrt(jnp.mean(o * o, axis=-1, keepdims=True) + EPS) * nw_ref[...]
        o_ref[:, h * hd:(h + 1) * hd] = o * _silu(z[:, h * hd:(h + 1) * hd])


def _gdn_branch(p, off, conv_w, a_log, dt_bias, norm_w, t_blk, heads_per_step):
    s = p.shape[0]
    hd = GDN_HEAD_DIM
    nh = GDN_HEADS
    bwid = heads_per_step * hd
    ngrp = nh // heads_per_step
    qb, kb, vb, zb = (off[n] // bwid for n in ("gdn_q", "gdn_k", "gdn_v", "gdn_z"))
    smb = off["small"] // SMALL_W
    blk = lambda base: pl.BlockSpec((t_blk, bwid), lambda h, t: (t, base + h))
    cwb = lambda base: pl.BlockSpec((CONV_W, bwid), lambda h, t: (0, base + h))
    return pl.pallas_call(
        _gdn_kernel,
        grid=(ngrp, s // t_blk),
        in_specs=[blk(qb), blk(kb), blk(vb), blk(zb),
                  pl.BlockSpec((t_blk, SMALL_W), lambda h, t: (t, smb)),
                  cwb(0), cwb(ngrp), cwb(2 * ngrp),
                  pl.BlockSpec((1, nh), lambda h, t: (0, 0)),
                  pl.BlockSpec((1, nh), lambda h, t: (0, 0)),
                  pl.BlockSpec((1, hd), lambda h, t: (0, 0))],
        out_specs=pl.BlockSpec((t_blk, bwid), lambda h, t: (t, h)),
        out_shape=jax.ShapeDtypeStruct((s, nh * hd), F32),
        scratch_shapes=[pltpu.VMEM((SUBLANES, bwid), F32)] * 3 + [pltpu.VMEM((heads_per_step, hd, hd), F32)],
        compiler_params=_params("arbitrary", "arbitrary"),
    )(p, p, p, p, p, conv_w, conv_w, conv_w, a_log, dt_bias, norm_w)


def _rwkv_kernel(r_ref, k_ref, v_ref, sm_ref, mur_ref, muk_ref, muv_ref, mus_ref, w0_ref, a0_ref, kk_ref, ka_ref,
                 rk_ref, lnw_ref, lnb_ref, wup_ref, aup_ref, gup_ref,
                 o_ref, tr_ref, tk_ref, tv_ref, ts_ref, st_ref):
    t_blk, pw = r_ref.shape
    hd = RWKV_HEAD
    nh = pw // hd

    @pl.when(pl.program_id(1) == 0)
    def _():
        tr_ref[...] = jnp.zeros_like(tr_ref)
        tk_ref[...] = jnp.zeros_like(tk_ref)
        tv_ref[...] = jnp.zeros_like(tv_ref)
        ts_ref[...] = jnp.zeros_like(ts_ref)
        st_ref[...] = jnp.zeros_like(st_ref)

    def token_shift(x_ref, tail_ref, mu_ref):
        x = x_ref[...]
        prev = _shift_rows(x, tail_ref, 1)
        tail_ref[...] = x[t_blk - SUBLANES:, :]
        return x + (prev - x) * mu_ref[...]

    r = token_shift(r_ref, tr_ref, mur_ref)
    k = token_shift(k_ref, tk_ref, muk_ref)
    v = token_shift(v_ref, tv_ref, muv_ref)
    sm = token_shift(sm_ref, ts_ref, mus_ref)

    lo = sm[:, SM_WLO:SM_WLO + LANES]
    w = -_softplus(-(w0_ref[...] + _dot(jnp.tanh(lo), wup_ref[...]))) - 0.5
    a = _sigmoid(a0_ref[...] + _dot(lo, aup_ref[...]))
    g = _dot(_sigmoid(sm[:, SM_GLO:SM_GLO + SM_GLO_PAD]), gup_ref[...])
    log_dec = -jnp.exp(w)

    lane = lax.broadcasted_iota(jnp.int32, (t_blk, pw), 1)

    def head_sum(x):
        tot = jnp.zeros_like(x)
        for h in range(nh):
            sh = jnp.sum(x[:, h * hd:(h + 1) * hd], axis=1, keepdims=True)
            tot = jnp.where((lane >= h * hd) & (lane < (h + 1) * hd), sh, tot)
        return tot

    kraw = k * kk_ref[...]
    kk = kraw * lax.rsqrt(head_sum(kraw * kraw) + EPS)
    k2 = k * (1.0 + (a - 1.0) * ka_ref[...])

    gcs = _seg_cumsum(log_dec, CHUNK)
    e_inc = jnp.exp(gcs)
    e_exc = jnp.exp(gcs - log_dec)
    e_neg = jnp.exp(-gcs)
    xa = -kk * e_exc
    xr = r * e_inc
    yb = kk * a * e_neg
    yk = k2 * e_neg

    causal, strict, _ = _tri_masks(CHUNK)
    mask_sc = jnp.concatenate([strict, causal], axis=0)
    n_chunks = t_blk // CHUNK
    items = [(h, c) for h in range(nh) for c in range(n_chunks)]

    def sl(arr, h, c):
        return arr[c * CHUNK:(c + 1) * CHUNK, h * hd:(h + 1) * hd]

    ms = [_dot_nt(jnp.concatenate([sl(xa, h, c), sl(xr, h, c)], axis=0),
                  jnp.concatenate([sl(yb, h, c), sl(yk, h, c)], axis=0)) for h, c in items]
    nil = [jnp.where(strict, m[:CHUNK, :CHUNK], 0.0) for m in ms]
    a_rb = [jnp.where(causal, m[CHUNK:, :CHUNK], 0.0) for m in ms]
    avk = [_dot(jnp.where(mask_sc, m[:, CHUNK:], 0.0), sl(v, h, c)) for m, (h, c) in zip(ms, items)]
    xs = [jnp.concatenate([sl(xa, h, c), av[:CHUNK]], axis=1) for av, (h, c) in zip(avk, items)]
    xs = _neumann_solve(nil, xs)
    tn_b = [_dot_tn(x, sl(yb, h, c)) for x, (h, c) in zip(xs, items)]
    tn_k = [_dot_tn(sl(v, h, c), sl(yk, h, c)) for h, c in items]
    rx = [_dot(a_, x) for a_, x in zip(a_rb, xs)]
    q_eff = [sl(xr, h, c) + r_[:, :hd] for r_, (h, c) in zip(rx, items)]
    o_const = [r_[:, hd:] + av[CHUNK:] for r_, av in zip(rx, avk)]

    states = [st_ref[h] for h in range(nh)]
    outs = [[] for _ in range(nh)]
    for c in range(n_chunks):
        for h in range(nh):
            i = h * n_chunks + c
            outs[h].append(_dot_nt(q_eff[i], states[h]) + o_const[i])
        new_states = []
        for h in range(nh):
            i = h * n_chunks + c
            d_last = e_inc[(c + 1) * CHUNK - 1:(c + 1) * CHUNK, h * hd:(h + 1) * hd]
            new_states.append((states[h] + _dot(states[h], tn_b[i][:hd]) + tn_b[i][hd:] + tn_k[i]) * d_last)
        states = new_states
    for h in range(nh):
        st_ref[h] = states[h]
    out = jnp.concatenate([jnp.concatenate(o, axis=0) for o in outs], axis=1)

    inv_n = 1.0 / hd
    mean = head_sum(out) * inv_n
    cen = out - mean
    var = head_sum(cen * cen) * inv_n
    gn = cen * lax.rsqrt(var + RWKV_GN_EPS) * lnw_ref[...] + lnb_ref[...]
    bonus = head_sum(r * k2 * rk_ref[...]) * v
    o_ref[...] = (gn + bonus) * g


def _rwkv_branch(p, off, mu_r, mu_k, mu_v, mu_s, w0, a0, k_k, k_a, r_k, ln_w, ln_b, w_up, a_up, g_up, t_blk,
                 heads_per_step):
    s = p.shape[0]
    pw = heads_per_step * RWKV_HEAD
    width = w0.shape[1]
    ngrp = width // pw
    rb, kb, vb = (off[n] // pw for n in ("rw_r", "rw_k", "rw_v"))
    smb = off["small"] // SMALL_W
    blk = lambda base: pl.BlockSpec((t_blk, pw), lambda h, t: (t, base + h))
    vec = pl.BlockSpec((1, pw), lambda h, t: (0, h))
    return pl.pallas_call(
        _rwkv_kernel,
        grid=(ngrp, s // t_blk),
        in_specs=[blk(rb), blk(kb), blk(vb),
                  pl.BlockSpec((t_blk, SMALL_W), lambda h, t: (t, smb)),
                  vec, vec, vec,
                  pl.BlockSpec((1, SMALL_W), lambda h, t: (0, 0)),
                  vec, vec, vec, vec, vec, vec, vec,
                  pl.BlockSpec((LANES, pw), lambda h, t: (0, h)),
                  pl.BlockSpec((LANES, pw), lambda h, t: (0, h)),
                  pl.BlockSpec((SM_GLO_PAD, pw), lambda h, t: (0, h))],
        out_specs=pl.BlockSpec((t_blk, pw), lambda h, t: (t, h)),
        out_shape=jax.ShapeDtypeStruct((s, width), F32),
        scratch_shapes=[pltpu.VMEM((SUBLANES, pw), F32)] * 3 + [pltpu.VMEM((SUBLANES, SMALL_W), F32),
                                                               pltpu.VMEM((heads_per_step, RWKV_HEAD, RWKV_HEAD), F32)],
        compiler_params=_params("arbitrary", "arbitrary"),
    )(p, p, p, p, mu_r, mu_k, mu_v, mu_s, w0, a0, k_k, k_a, r_k, ln_w, ln_b, w_up, a_up, g_up)


def _merge_kernel(ya_ref, yb_ref, yc_ref, ga_ref, gb_ref, gc_ref, w_ref, o_ref):
    acc = None
    for b, (y_ref, g_ref) in enumerate(((ya_ref, ga_ref), (yb_ref, gb_ref), (yc_ref, gc_ref))):
        term = _sigmoid(g_ref[...]) * _dot(y_ref[...].astype(BF16), w_ref[b])
        acc = term if acc is None else acc + term
    o_ref[...] = acc.astype(o_ref.dtype)


def _merge(ya, yb, yc, p, off, w_branch, tm, tn):
    s, bw = ya.shape
    d = w_branch.shape[2]
    gbase = off["gates"] // tn
    nj = d // tn
    yspec = pl.BlockSpec((tm, bw), lambda i, j: (i, 0))
    gspec = lambda b: pl.BlockSpec((tm, tn), lambda i, j: (i, gbase + b * nj + j))
    return pl.pallas_call(
        _merge_kernel,
        grid=(s // tm, nj),
        in_specs=[yspec, yspec, yspec, gspec(0), gspec(1), gspec(2),
                  pl.BlockSpec((N_BRANCH, bw, tn), lambda i, j: (0, 0, j))],
        out_specs=pl.BlockSpec((tm, tn), lambda i, j: (i, j)),
        out_shape=jax.ShapeDtypeStruct((s, d), BF16),
        compiler_params=_params("arbitrary", "arbitrary"),
    )(ya, yb, yc, p, p, p, w_branch)


def _out_proj_kernel(m_ref, w_ref, nw_ref, x_ref, o_ref):
    h = _dot(m_ref[...], w_ref[...])
    ms = jnp.mean(h * h, axis=-1, keepdims=True)
    o_ref[...] = x_ref[...] + h * lax.rsqrt(ms + EPS) * nw_ref[...]


def _out_proj(mixed, w_out, nw, x, tm):
    s, d = x.shape
    return pl.pallas_call(
        _out_proj_kernel,
        grid=(s // tm,),
        in_specs=[pl.BlockSpec((tm, d), lambda i: (i, 0)),
                  pl.BlockSpec((d, d), lambda i: (0, 0)),
                  pl.BlockSpec((1, d), lambda i: (0, 0)),
                  pl.BlockSpec((tm, d), lambda i: (i, 0))],
        out_specs=pl.BlockSpec((tm, d), lambda i: (i, 0)),
        out_shape=jax.ShapeDtypeStruct((s, d), F32),
        compiler_params=_params("arbitrary"),
    )(mixed, w_out, nw, x)


def _ffn_kernel(x_ref, npre_ref, npost_ref, wg_ref, wu_ref, wd_ref, o_ref, h_ref, acc_ref):
    kk = pl.program_id(1)

    @pl.when(kk == 0)
    def _():
        x = x_ref[...]
        ms = jnp.mean(x * x, axis=-1, keepdims=True)
        h_ref[...] = (x * lax.rsqrt(ms + EPS) * npre_ref[...]).astype(BF16)
        acc_ref[...] = jnp.zeros_like(acc_ref)

    h = h_ref[...]
    act = (_silu(_dot(h, wg_ref[...])) * _dot(h, wu_ref[...])).astype(BF16)
    acc_ref[...] += _dot(act, wd_ref[...])

    @pl.when(kk == pl.num_programs(1) - 1)
    def _():
        y = acc_ref[...]
        ms = jnp.mean(y * y, axis=-1, keepdims=True)
        o_ref[...] = x_ref[...] + y * lax.rsqrt(ms + EPS) * npost_ref[...]


def _ffn(x, npre, npost, wg, wu, wd, tm, tf):
    s, d = x.shape
    f = wg.shape[1]
    return pl.pallas_call(
        _ffn_kernel,
        grid=(s // tm, f // tf),
        in_specs=[pl.BlockSpec((tm, d), lambda i, k: (i, 0)),
                  pl.BlockSpec((1, d), lambda i, k: (0, 0)),
                  pl.BlockSpec((1, d), lambda i, k: (0, 0)),
                  pl.BlockSpec((d, tf), lambda i, k: (0, k)),
                  pl.BlockSpec((d, tf), lambda i, k: (0, k)),
                  pl.BlockSpec((tf, d), lambda i, k: (k, 0))],
        out_specs=pl.BlockSpec((tm, d), lambda i, k: (i, 0)),
        out_shape=jax.ShapeDtypeStruct((s, d), F32),
        scratch_shapes=[pltpu.VMEM((tm, d), BF16), pltpu.VMEM((tm, d), F32)],
        compiler_params=_params("arbitrary", "arbitrary"),
    )(x, npre, npost, wg, wu, wd)


def _split_w_in(w_in, d_model, bw):
    sizes = (bw, bw, 3 * bw, bw, GDN_HEADS, GDN_HEADS,
             3 * bw + RWKV_W_RANK + RWKV_A_RANK + RWKV_G_RANK, N_BRANCH * d_model)
    idx = [int(i) for i in np.cumsum(sizes)[:-1]]
    return jnp.split(w_in, idx, axis=-1)


def _permute_w_in(w_in, d_model, bw, total):
    lru_x, lru_g, qkv, z, beta, alpha, rw, gates = _split_w_in(w_in, d_model, bw)
    rkv, lo = rw[:, :3 * bw], rw[:, 3 * bw:]
    small = jnp.concatenate([lo, beta, alpha], axis=1)
    used = 9 * bw + N_BRANCH * d_model + small.shape[1]
    pad = jnp.zeros((w_in.shape[0], total - used), w_in.dtype)
    return jnp.concatenate([lru_x, lru_g, qkv, z, rkv, gates, small, pad], axis=1).astype(BF16)


def _block_diag_gates(w_a, w_x, group):
    nblk, bs, _ = w_a.shape
    ng = nblk // group
    gw = group * bs
    out = jnp.zeros((ng, gw, 2 * gw), F32)
    for b in range(nblk):
        gi, bi = divmod(b, group)
        out = out.at[gi, bi * bs:(bi + 1) * bs, bi * bs:(bi + 1) * bs].set(w_a[b])
        out = out.at[gi, bi * bs:(bi + 1) * bs, gw + bi * bs:gw + (bi + 1) * bs].set(w_x[b])
    return out


def _pad_rows(w, rows, at):
    out = jnp.zeros((rows, w.shape[1]), w.dtype)
    return out.at[at:at + w.shape[0]].set(w)


def _tiles(s):
    return dict(tm_in=min(s, 1024), tn_in=512, t_lru=min(s, 256),
                t_gdn=min(s, 256), hb_gdn=4, t_rwkv=min(s, 256), hb_rwkv=4,
                tm_merge=min(s, 512), tn_merge=1024, tm_out=min(s, 512), tm_ffn=min(s, 512), tf_ffn=512)


def _layer(x, lp, tiles):
    s, d = x.shape
    bw = lp["lru_conv_w"].shape[1]
    off = _layout(d, bw)
    row = lambda v: v.reshape(1, -1)

    p = _norm_matmul(x, row(lp["norm_mix_pre"]), lp["w_in_perm"], tiles["tm_in"], tiles["tn_in"])

    y_a = _lru_branch(p, off, lp["lru_conv_w"], row(lp["lru_conv_b"]), lp["lru_wax"], row(lp["lru_b_a"]),
                      row(lp["lru_b_x"]), row(lp["lru_lambda"]), tiles["t_lru"])
    y_b = _gdn_branch(p, off, lp["gdn_conv_w"], row(lp["gdn_a_log"]), row(lp["gdn_dt_bias"]),
                      row(lp["gdn_norm_w"]), tiles["t_gdn"], tiles["hb_gdn"])
    mu = lp["rwkv_mu"]
    mu_s = jnp.zeros((1, SMALL_W), F32).at[0, :SM_BETA].set(mu[3 * bw:])
    y_c = _rwkv_branch(p, off, row(mu[:bw]), row(mu[bw:2 * bw]), row(mu[2 * bw:3 * bw]), mu_s,
                       row(lp["rwkv_w0"]), row(lp["rwkv_a0"]), row(lp["rwkv_k_k"]), row(lp["rwkv_k_a"]),
                       row(lp["rwkv_r_k"]), row(lp["rwkv_ln_w"]), row(lp["rwkv_ln_b"]),
                       _pad_rows(lp["rwkv_w_up"], LANES, SM_WLO), _pad_rows(lp["rwkv_a_up"], LANES, SM_ALO),
                       _pad_rows(lp["rwkv_g_up"], SM_GLO_PAD, 0), tiles["t_rwkv"], tiles["hb_rwkv"])

    mixed = _merge(y_a, y_b, y_c, p, off, lp["w_branch"], tiles["tm_merge"], tiles["tn_merge"])
    x = _out_proj(mixed, lp["w_out"], row(lp["norm_mix_post"]), x, tiles["tm_out"])
    x = _ffn(x, row(lp["norm_ffn_pre"]), row(lp["norm_ffn_post"]), lp["ffn_w_gate"], lp["ffn_w_up"],
             lp["ffn_w_down"], tiles["tm_ffn"], tiles["tf_ffn"])
    return x


def kernel(x, norm_mix_pre, norm_mix_post, norm_ffn_pre, norm_ffn_post, w_in, lru_conv_w, lru_conv_b, lru_w_a, lru_b_a, lru_w_x, lru_b_x, lru_lambda, gdn_conv_w, gdn_a_log, gdn_dt_bias, gdn_norm_w, rwkv_mu, rwkv_w0, rwkv_w_up, rwkv_a0, rwkv_a_up, rwkv_g_up, rwkv_k_k, rwkv_k_a, rwkv_r_k, rwkv_ln_w, rwkv_ln_b, w_branch, w_out, ffn_w_gate, ffn_w_up, ffn_w_down):
    batch, s, d = x.shape
    depth = w_in.shape[0]
    bw = lru_conv_w.shape[2]
    total = _layout(d, bw)["total"]
    tiles = _tiles(s)
    outs = []
    for b in range(batch):
        xb = x[b]
        for l in range(depth):
            lp = dict(
                norm_mix_pre=norm_mix_pre[l], norm_mix_post=norm_mix_post[l],
                norm_ffn_pre=norm_ffn_pre[l], norm_ffn_post=norm_ffn_post[l],
                w_in_perm=_permute_w_in(w_in[l], d, bw, total),
                lru_conv_w=lru_conv_w[l], lru_conv_b=lru_conv_b[l],
                lru_wax=_block_diag_gates(lru_w_a[l], lru_w_x[l], 2 * LANES // LRU_BLOCK),
                lru_b_a=lru_b_a[l], lru_b_x=lru_b_x[l], lru_lambda=lru_lambda[l],
                gdn_conv_w=gdn_conv_w[l], gdn_a_log=gdn_a_log[l], gdn_dt_bias=gdn_dt_bias[l],
                gdn_norm_w=gdn_norm_w[l],
                rwkv_mu=rwkv_mu[l], rwkv_w0=rwkv_w0[l], rwkv_w_up=rwkv_w_up[l], rwkv_a0=rwkv_a0[l],
                rwkv_a_up=rwkv_a_up[l], rwkv_g_up=rwkv_g_up[l], rwkv_k_k=rwkv_k_k[l], rwkv_k_a=rwkv_k_a[l],
                rwkv_r_k=rwkv_r_k[l], rwkv_ln_w=rwkv_ln_w[l], rwkv_ln_b=rwkv_ln_b[l],
                w_branch=w_branch[l].astype(BF16), w_out=w_out[l].astype(BF16),
                ffn_w_gate=ffn_w_gate[l].astype(BF16), ffn_w_up=ffn_w_up[l].astype(BF16),
                ffn_w_down=ffn_w_down[l].astype(BF16))
            xb = _layer(xb, lp, tiles)
        outs.append(xb)
    return jnp.stack(outs, axis=0)
```

```python
import jax
import jax.numpy as jnp
import numpy as np
from jax import lax
from jax.experimental import pallas as pl
from jax.experimental.pallas import tpu as pltpu

F32 = jnp.float32
BF16 = jnp.bfloat16

EPS = 1e-6
LRU_C = 8.0
LRU_BLOCK = 64
GDN_HEADS = 8
GDN_HEAD_DIM = 128
RWKV_HEAD = 64
RWKV_W_RANK = 64
RWKV_A_RANK = 64
RWKV_G_RANK = 160
RWKV_GN_EPS = 64e-5
N_BRANCH = 3
CONV_W = 4

CHUNK = 64
LANES = 128
SUBLANES = 8
VMEM_LIMIT = 56 * 1024 * 1024

SMALL_W = 512


def _layout(d_model, branch_w):
    off = {}
    c = 0
    for name, width in (("gates", N_BRANCH * d_model), ("lru_x", branch_w), ("lru_g", branch_w),
                        ("gdn_q", branch_w), ("gdn_k", branch_w), ("gdn_v", branch_w), ("gdn_z", branch_w),
                        ("rw_r", branch_w), ("rw_k", branch_w), ("rw_v", branch_w), ("small", SMALL_W)):
        off[name] = c
        c += width
    off["total"] = c
    return off


SM_WLO = 0
SM_ALO = SM_WLO + RWKV_W_RANK
SM_GLO = SM_ALO + RWKV_A_RANK
SM_BETA = SM_GLO + RWKV_G_RANK
SM_ALPHA = SM_BETA + GDN_HEADS
SM_USED = SM_ALPHA + GDN_HEADS
SM_GLO_PAD = 256
SM_PAR = (SM_BETA // LANES) * LANES
assert SM_PAR <= SM_BETA and SM_USED <= SM_PAR + LANES


def _params(*sem):
    return pltpu.CompilerParams(dimension_semantics=sem, vmem_limit_bytes=VMEM_LIMIT)


def _dot(a, b):
    return jnp.dot(a, b, preferred_element_type=F32)


def _dot_nt(a, b):
    return lax.dot_general(a, b, (((1,), (1,)), ((), ())), preferred_element_type=F32)


def _dot_tn(a, b):
    return lax.dot_general(a, b, (((0,), (0,)), ((), ())), preferred_element_type=F32)


def _softplus(x):
    return jnp.maximum(x, 0.0) + jnp.log1p(jnp.exp(-jnp.abs(x)))


def _sigmoid(x):
    return 1.0 / (1.0 + jnp.exp(-x))


def _silu(x):
    return x * _sigmoid(x)


def _gelu_tanh(x):
    c = np.float32(np.sqrt(2.0 / np.pi))
    return 0.5 * x * (1.0 + jnp.tanh(c * (x + 0.044715 * (x * x * x))))


def _tri_masks(n):
    row = lax.broadcasted_iota(jnp.int32, (n, n), 0)
    col = lax.broadcasted_iota(jnp.int32, (n, n), 1)
    return row >= col, row > col, row == col


def _neumann_solve(nils, xs):
    steps = int(np.log2(CHUNK))
    for it in range(steps):
        xs = [x + _dot(p, x) for p, x in zip(nils, xs)]
        if it + 1 < steps:
            nils = [_dot(p, p) for p in nils]
    return xs


def _seg_cumsum(x, seg):
    row = lax.broadcasted_iota(jnp.int32, x.shape, 0)
    pos = row & (seg - 1)
    k = 1
    while k < seg:
        x = jnp.where(pos >= k, x + pltpu.roll(x, k, axis=0), x)
        k *= 2
    return x


def _stage_rows(buf_ref, x):
    buf_ref[SUBLANES:, :] = x


def _rows_before(buf_ref, t, k):
    return buf_ref[pl.ds(SUBLANES - k, t), :]


def _keep_tail(buf_ref, t):
    buf_ref[0:SUBLANES, :] = buf_ref[pl.ds(t, SUBLANES), :]


def _zero_tail(buf_ref):
    buf_ref[0:SUBLANES, :] = jnp.zeros((SUBLANES, buf_ref.shape[1]), buf_ref.dtype)


def _norm_matmul_kernel(x_ref, nw_ref, w_ref, o_ref, h_ref):
    @pl.when(pl.program_id(1) == 0)
    def _():
        x = x_ref[...]
        ms = jnp.mean(x * x, axis=-1, keepdims=True)
        h_ref[...] = (x * lax.rsqrt(ms + EPS) * nw_ref[...]).astype(BF16)

    o_ref[...] = _dot(h_ref[...], w_ref[...])


def _norm_matmul(x, nw, w, tm, tn):
    s, d = x.shape
    n = w.shape[1]
    return pl.pallas_call(
        _norm_matmul_kernel,
        grid=(s // tm, n // tn),
        in_specs=[pl.BlockSpec((tm, d), lambda i, j: (i, 0)),
                  pl.BlockSpec((1, d), lambda i, j: (0, 0)),
                  pl.BlockSpec((d, tn), lambda i, j: (0, j))],
        out_specs=pl.BlockSpec((tm, tn), lambda i, j: (i, j)),
        out_shape=jax.ShapeDtypeStruct((s, n), F32),
        scratch_shapes=[pltpu.VMEM((tm, d), BF16)],
        compiler_params=_params("arbitrary", "arbitrary"),
    )(x, nw, w)


def _lru_kernel(x_ref, g_ref, cw_ref, cb_ref, wax_ref, ba_ref, bx_ref, lam_ref, o_ref, tail_ref, hc_ref):
    t_blk = x_ref.shape[0]
    width = x_ref.shape[1]

    @pl.when(pl.program_id(0) == 0)
    def _():
        _zero_tail(tail_ref)
        hc_ref[...] = jnp.zeros_like(hc_ref)

    x = x_ref[...]
    cw = cw_ref[...]
    _stage_rows(tail_ref, x)
    xc = x * cw[3:4] + cb_ref[...]
    for k in range(1, CONV_W):
        xc = xc + _rows_before(tail_ref, t_blk, k) * cw[CONV_W - 1 - k:CONV_W - k]
    _keep_tail(tail_ref, t_blk)

    nb = wax_ref.shape[0]
    gw = width // nb
    r_parts, i_parts = [], []
    for b in range(nb):
        ri = _dot(xc[:, gw * b:gw * (b + 1)], wax_ref[b])
        r_parts.append(ri[:, :gw])
        i_parts.append(ri[:, gw:])
    r = _sigmoid(jnp.concatenate(r_parts, axis=1) + ba_ref[...])
    i = _sigmoid(jnp.concatenate(i_parts, axis=1) + bx_ref[...])
    log_a = (-LRU_C) * r * _softplus(-lam_ref[...])
    a = jnp.exp(log_a)
    u = jnp.sqrt(1.0 - jnp.exp(2.0 * log_a)) * (i * xc)

    row = lax.broadcasted_iota(jnp.int32, a.shape, 0)
    k = 1
    while k < t_blk:
        m = row >= k
        u = jnp.where(m, a * pltpu.roll(u, k, axis=0) + u, u)
        a = jnp.where(m, a * pltpu.roll(a, k, axis=0), a)
        k *= 2
    h = u + a * hc_ref[0:1, :]
    hc_ref[...] = jnp.broadcast_to(h[t_blk - 1:t_blk, :], hc_ref.shape)
    o_ref[...] = _gelu_tanh(g_ref[...]) * h


def _lru_branch(p, off, cw, cb, wax, ba, bx, lam, t_blk):
    s = p.shape[0]
    width = cw.shape[1]
    xb = off["lru_x"] // width
    gb = off["lru_g"] // width
    vec = pl.BlockSpec((1, width), lambda t: (0, 0))
    return pl.pallas_call(
        _lru_kernel,
        grid=(s // t_blk,),
        in_specs=[pl.BlockSpec((t_blk, width), lambda t: (t, xb)),
                  pl.BlockSpec((t_blk, width), lambda t: (t, gb)),
                  pl.BlockSpec((CONV_W, width), lambda t: (0, 0)),
                  vec,
                  pl.BlockSpec(wax.shape, lambda t: (0, 0, 0)),
                  vec, vec, vec],
        out_specs=pl.BlockSpec((t_blk, width), lambda t: (t, 0)),
        out_shape=jax.ShapeDtypeStruct((s, width), F32),
        scratch_shapes=[pltpu.VMEM((t_blk + SUBLANES, width), F32), pltpu.VMEM((SUBLANES, width), F32)],
        compiler_params=_params("arbitrary"),
    )(p, p, cw, cb, wax, ba, bx, lam)


def _gdn_kernel(q_ref, k_ref, v_ref, z_ref, sm_ref, cwq_ref, cwk_ref, cwv_ref, alog_ref, dtb_ref, nw_ref,
                o_ref, tq_ref, tk_ref, tv_ref, st_ref):
    t_blk = q_ref.shape[0]
    hd = GDN_HEAD_DIM
    nh = q_ref.shape[1] // hd
    n_chunks = t_blk // CHUNK
    head0 = pl.program_id(0) * nh

    @pl.when(pl.program_id(1) == 0)
    def _():
        _zero_tail(tq_ref)
        _zero_tail(tk_ref)
        _zero_tail(tv_ref)
        st_ref[...] = jnp.zeros_like(st_ref)

    def conv_silu(x_ref, tail_ref, cw_ref):
        x = x_ref[...]
        cw = cw_ref[...]
        _stage_rows(tail_ref, x)
        y = x * cw[3:4]
        for k in range(1, CONV_W):
            y = y + _rows_before(tail_ref, t_blk, k) * cw[CONV_W - 1 - k:CONV_W - k]
        _keep_tail(tail_ref, t_blk)
        return _silu(y)

    q_all = conv_silu(q_ref, tq_ref, cwq_ref)
    k_all = conv_silu(k_ref, tk_ref, cwk_ref)
    v_all = conv_silu(v_ref, tv_ref, cwv_ref)

    par = sm_ref[...]
    lane = lax.broadcasted_iota(jnp.int32, (1, LANES), 1)
    beta_all = _sigmoid(par)
    gc_all = _seg_cumsum(-jnp.exp(alog_ref[...]) * _softplus(par + dtb_ref[...]), CHUNK)
    causal, strict, eye = _tri_masks(CHUNK)

    qs, ks, vs, betas, gcs, egcs = [], [], [], [], [], []
    for h in range(nh):
        hs = slice(h * hd, (h + 1) * hd)
        q = q_all[:, hs]
        k = k_all[:, hs]
        qs.append(q * lax.rsqrt(jnp.sum(q * q, axis=-1, keepdims=True) + EPS) * (hd ** -0.5))
        ks.append(k * lax.rsqrt(jnp.sum(k * k, axis=-1, keepdims=True) + EPS))
        vs.append(v_all[:, hs])
        beta = jnp.sum(jnp.where(lane == SM_BETA - SM_PAR + head0 + h, beta_all, 0.0), axis=1, keepdims=True)
        gc = jnp.sum(jnp.where(lane == SM_ALPHA - SM_PAR + head0 + h, gc_all, 0.0), axis=1, keepdims=True)
        betas.append(jnp.broadcast_to(beta, (t_blk, hd)))
        gc = jnp.broadcast_to(gc, (t_blk, hd))
        gcs.append(gc)
        egcs.append(jnp.exp(gc))

    items = [(h, c) for h in range(nh) for c in range(n_chunks)]

    def rows(arrs, h, c):
        return arrs[h][c * CHUNK:(c + 1) * CHUNK]

    kq = [_dot_nt(jnp.concatenate([rows(ks, h, c), rows(qs, h, c)], axis=0), rows(ks, h, c)) for h, c in items]
    nil, attn, xs = [], [], []
    for (h, c), m in zip(items, kq):
        gcc, bc, kc = rows(gcs, h, c), rows(betas, h, c), rows(ks, h, c)
        g_col = gcc[:, :CHUNK]
        g_row = jnp.sum(jnp.where(eye, g_col, 0.0), axis=0, keepdims=True)
        decay = jnp.exp(jnp.where(causal, g_col - g_row, 0.0))
        nil.append(jnp.where(strict, -(bc[:, :CHUNK] * m[:CHUNK] * decay), 0.0))
        attn.append(jnp.where(causal, m[CHUNK:] * decay, 0.0))
        xs.append(jnp.concatenate([bc * rows(vs, h, c), bc * rows(egcs, h, c) * kc], axis=1))
    xs = _neumann_solve(nil, xs)
    g_last = [rows(gcs, h, c)[CHUNK - 1:CHUNK, :] for h, c in items]
    tn = [_dot_tn(rows(ks, h, c) * jnp.exp(gl - rows(gcs, h, c)), x) for (h, c), gl, x in zip(items, g_last, xs)]
    ax = [_dot(a_, x) for a_, x in zip(attn, xs)]
    q_eff = [rows(qs, h, c) * rows(egcs, h, c) - a_[:, hd:] for (h, c), a_ in zip(items, ax)]
    lhs = [jnp.concatenate([t_[:, hd:], q_], axis=0) for t_, q_ in zip(tn, q_eff)]
    e_last = [jnp.exp(gl) for gl in g_last]

    states = [st_ref[h] for h in range(nh)]
    outs = [[] for _ in range(nh)]
    for c in range(n_chunks):
        res = [_dot(lhs[h * n_chunks + c], states[h]) for h in range(nh)]
        for h in range(nh):
            i = h * n_chunks + c
            states[h] = states[h] * e_last[i] - res[h][:hd] + tn[i][:, :hd]
            outs[h].append(res[h][hd:] + ax[i][:, :hd])
    for h in range(nh):
        st_ref[h] = states[h]

    z = z_ref[...]
    for h in range(nh):
        o = jnp.concatenate(outs[h], axis=0)
        o = o * lax.rsqrt(jnp.mean(o * o, axis=-1, keepdims=True) + EPS) * nw_ref[...]
        o_ref[:, h * hd:(h + 1) * hd] = o * _silu(z[:, h * hd:(h + 1) * hd])


def _gdn_branch(p, off, conv_w, a_log, dt_bias, norm_w, t_blk, heads_per_step):
    s = p.shape[0]
    hd = GDN_HEAD_DIM
    nh = GDN_HEADS
    bwid = heads_per_step * hd
    ngrp = nh // heads_per_step
    qb, kb, vb, zb = (off[n] // bwid for n in ("gdn_q", "gdn_k", "gdn_v", "gdn_z"))
    parb = (off["small"] + SM_PAR) // LANES
    blk = lambda base: pl.BlockSpec((t_blk, bwid), lambda h, t: (t, base + h))
    cwb = lambda base: pl.BlockSpec((CONV_W, bwid), lambda h, t: (0, base + h))
    lane_row = pl.BlockSpec((1, LANES), lambda h, t: (0, 0))
    at_alpha = lambda v: jnp.pad(v, ((0, 0), (SM_ALPHA - SM_PAR, LANES - (SM_ALPHA - SM_PAR) - nh)))
    return pl.pallas_call(
        _gdn_kernel,
        grid=(ngrp, s // t_blk),
        in_specs=[blk(qb), blk(kb), blk(vb), blk(zb),
                  pl.BlockSpec((t_blk, LANES), lambda h, t: (t, parb)),
                  cwb(0), cwb(ngrp), cwb(2 * ngrp),
                  lane_row, lane_row,
                  pl.BlockSpec((1, hd), lambda h, t: (0, 0))],
        out_specs=pl.BlockSpec((t_blk, bwid), lambda h, t: (t, h)),
        out_shape=jax.ShapeDtypeStruct((s, nh * hd), F32),
        scratch_shapes=([pltpu.VMEM((t_blk + SUBLANES, bwid), F32)] * 3
                        + [pltpu.VMEM((heads_per_step, hd, hd), F32)]),
        compiler_params=_params("arbitrary", "arbitrary"),
    )(p, p, p, p, p, conv_w, conv_w, conv_w, at_alpha(a_log), at_alpha(dt_bias), norm_w)


def _rwkv_kernel(r_ref, k_ref, v_ref, sm_ref, mur_ref, muk_ref, muv_ref, mus_ref, w0_ref, a0_ref, kk_ref, ka_ref,
                 rk_ref, lnw_ref, lnb_ref, wup_ref, aup_ref, gup_ref,
                 o_ref, tr_ref, tk_ref, tv_ref, ts_ref, st_ref):
    t_blk, pw = r_ref.shape
    hd = RWKV_HEAD
    nh = pw // hd

    @pl.when(pl.program_id(1) == 0)
    def _():
        _zero_tail(tr_ref)
        _zero_tail(tk_ref)
        _zero_tail(tv_ref)
        _zero_tail(ts_ref)
        st_ref[...] = jnp.zeros_like(st_ref)

    def token_shift(x_ref, tail_ref, mu_ref):
        x = x_ref[...]
        _stage_rows(tail_ref, x)
        prev = _rows_before(tail_ref, t_blk, 1)
        _keep_tail(tail_ref, t_blk)
        return x + (prev - x) * mu_ref[...]

    r = token_shift(r_ref, tr_ref, mur_ref)
    k = token_shift(k_ref, tk_ref, muk_ref)
    v = token_shift(v_ref, tv_ref, muv_ref)
    sm = token_shift(sm_ref, ts_ref, mus_ref)

    lo = sm[:, SM_WLO:SM_WLO + LANES]
    w = -_softplus(-(w0_ref[...] + _dot(jnp.tanh(lo), wup_ref[...]))) - 0.5
    a = _sigmoid(a0_ref[...] + _dot(lo, aup_ref[...]))
    g = _dot(_sigmoid(sm[:, SM_GLO:SM_GLO + SM_GLO_PAD]), gup_ref[...])
    log_dec = -jnp.exp(w)

    same_head = (lax.broadcasted_iota(jnp.int32, (pw, pw), 0) // hd) == (lax.broadcasted_iota(jnp.int32, (pw, pw), 1) // hd)
    ones_bd = jnp.where(same_head, 1.0, 0.0).astype(BF16)

    def head_sum(x):
        hi = x.astype(BF16)
        lo = (x - hi.astype(F32)).astype(BF16)
        return _dot(hi, ones_bd) + _dot(lo, ones_bd)

    kraw = k * kk_ref[...]
    kk = kraw * lax.rsqrt(head_sum(kraw * kraw) + EPS)
    k2 = k * (1.0 + (a - 1.0) * ka_ref[...])

    gcs = _seg_cumsum(log_dec, CHUNK)
    e_inc = jnp.exp(gcs)
    e_exc = jnp.exp(gcs - log_dec)
    e_neg = jnp.exp(-gcs)
    xa = -kk * e_exc
    xr = r * e_inc
    yb = kk * a * e_neg
    yk = k2 * e_neg

    n_chunks = t_blk // CHUNK
    npair = pw // LANES
    c2 = 2 * CHUNK
    lane1 = lax.broadcasted_iota(jnp.int32, (1, LANES), 1)
    own = [lane1 < hd, lane1 >= hd]
    causal, strict, _ = _tri_masks(CHUNK)
    r2 = lax.broadcasted_iota(jnp.int32, (c2, c2), 0)
    j2 = lax.broadcasted_iota(jnp.int32, (c2, c2), 1)
    mask2 = (r2 & (CHUNK - 1)) - (j2 & (CHUNK - 1)) + jnp.where(r2 < CHUNK, 0, 1) > 0
    eye2 = r2 == j2
    zeros_c = jnp.zeros((CHUNK, LANES), F32)

    def grp(arr, pp, c):
        return arr[c * CHUNK:(c + 1) * CHUNK, pp * LANES:(pp + 1) * LANES]

    v_sw = [pltpu.roll(v[:, pp * LANES:(pp + 1) * LANES], hd, axis=1) for pp in range(npair)]
    groups = [(pp, c) for pp in range(npair) for c in range(n_chunks)]
    items = [(pp, c, hh) for pp, c in groups for hh in range(2)]

    xcat = {g_: jnp.concatenate([grp(xa, *g_), grp(xr, *g_)], axis=0) for g_ in groups}
    ycat = {g_: jnp.concatenate([grp(yb, *g_), grp(yk, *g_)], axis=0) for g_ in groups}
    d_row = {(pp, c): e_inc[(c + 1) * CHUNK - 1:(c + 1) * CHUNK, pp * LANES:(pp + 1) * LANES] for pp, c in groups}
    ms = [_dot_nt(xcat[(pp, c)], jnp.where(own[hh], ycat[(pp, c)], 0.0)) for pp, c, hh in items]
    y_t = [jnp.where(own[hh], ycat[(pp, c)] * d_row[(pp, c)], 0.0).T for pp, c, hh in items]
    nil = [jnp.where(strict, m[:CHUNK, :CHUNK], 0.0) for m in ms]
    a_rb = [jnp.where(causal, m[CHUNK:, :CHUNK], 0.0) for m in ms]
    v_oth = [jnp.where(own[hh], 0.0, v_sw[pp][c * CHUNK:(c + 1) * CHUNK]) for pp, c, hh in items]
    avk = [_dot(jnp.where(mask2, m, 0.0), jnp.concatenate([zeros_c, vo], axis=0)) for m, vo in zip(ms, v_oth)]
    xs = [jnp.where(own[hh], grp(xa, pp, c), av[:CHUNK]) for av, (pp, c, hh) in zip(avk, items)]
    xs = _neumann_solve(nil, xs)
    tt = [_dot(yt, jnp.concatenate([x, vo], axis=0)) for yt, x, vo in zip(y_t, xs, v_oth)]
    rx = [_dot(a_, x) for a_, x in zip(a_rb, xs)]
    q_eff = [jnp.where(own[hh], grp(xr, pp, c) + r_, 0.0) for r_, (pp, c, hh) in zip(rx, items)]
    o_c = [r_ + av[CHUNK:] for r_, av in zip(rx, avk)]
    lhs, h_c, o_cc, d_col = {}, {}, {}, {}
    for gi, g_ in enumerate(groups):
        i0, i1 = 2 * gi, 2 * gi + 1
        lhs[g_] = jnp.concatenate([jnp.where(own[0], tt[i0], tt[i1]), q_eff[i0], q_eff[i1]], axis=0)
        h_c[g_] = jnp.where(own[0], tt[i1], tt[i0])
        o_cc[g_] = jnp.where(own[0], o_c[i1], o_c[i0])
        d_col[g_] = jnp.sum(jnp.where(eye2, d_row[g_], 0.0), axis=1, keepdims=True)

    states = [st_ref[pp] for pp in range(npair)]
    outs = [[] for _ in range(npair)]
    for c in range(n_chunks):
        res = [_dot(lhs[(pp, c)], states[pp]) for pp in range(npair)]
        for pp in range(npair):
            states[pp] = d_col[(pp, c)] * states[pp] + res[pp][:c2] + h_c[(pp, c)]
            outs[pp].append(res[pp][c2:c2 + CHUNK] + res[pp][c2 + CHUNK:] + o_cc[(pp, c)])
    for pp in range(npair):
        st_ref[pp] = states[pp]
    out = jnp.concatenate([pltpu.roll(jnp.concatenate(o, axis=0), hd, axis=1) for o in outs], axis=1)

    inv_n = 1.0 / hd
    mean = head_sum(out) * inv_n
    cen = out - mean
    var = head_sum(cen * cen) * inv_n
    gn = cen * lax.rsqrt(var + RWKV_GN_EPS) * lnw_ref[...] + lnb_ref[...]
    bonus = head_sum(r * k2 * rk_ref[...]) * v
    o_ref[...] = (gn + bonus) * g


def _rwkv_branch(p, off, mu_r, mu_k, mu_v, mu_s, w0, a0, k_k, k_a, r_k, ln_w, ln_b, w_up, a_up, g_up, t_blk,
                 heads_per_step):
    s = p.shape[0]
    pw = heads_per_step * RWKV_HEAD
    width = w0.shape[1]
    ngrp = width // pw
    rb, kb, vb = (off[n] // pw for n in ("rw_r", "rw_k", "rw_v"))
    smb = off["small"] // SMALL_W
    blk = lambda base: pl.BlockSpec((t_blk, pw), lambda h, t: (t, base + h))
    vec = pl.BlockSpec((1, pw), lambda h, t: (0, h))
    return pl.pallas_call(
        _rwkv_kernel,
        grid=(ngrp, s // t_blk),
        in_specs=[blk(rb), blk(kb), blk(vb),
                  pl.BlockSpec((t_blk, SMALL_W), lambda h, t: (t, smb)),
                  vec, vec, vec,
                  pl.BlockSpec((1, SMALL_W), lambda h, t: (0, 0)),
                  vec, vec, vec, vec, vec, vec, vec,
                  pl.BlockSpec((LANES, pw), lambda h, t: (0, h)),
                  pl.BlockSpec((LANES, pw), lambda h, t: (0, h)),
                  pl.BlockSpec((SM_GLO_PAD, pw), lambda h, t: (0, h))],
        out_specs=pl.BlockSpec((t_blk, pw), lambda h, t: (t, h)),
        out_shape=jax.ShapeDtypeStruct((s, width), F32),
        scratch_shapes=([pltpu.VMEM((t_blk + SUBLANES, pw), F32)] * 3
                        + [pltpu.VMEM((t_blk + SUBLANES, SMALL_W), F32),
                           pltpu.VMEM((pw // LANES, LANES, LANES), F32)]),
        compiler_params=_params("arbitrary", "arbitrary"),
    )(p, p, p, p, mu_r, mu_k, mu_v, mu_s, w0, a0, k_k, k_a, r_k, ln_w, ln_b, w_up, a_up, g_up)


def _merge_out_kernel(ya_ref, yb_ref, yc_ref, ga_ref, gb_ref, gc_ref, wb_ref, wo_ref, nw_ref, x_ref, o_ref):
    acc = None
    for b, (y_ref, g_ref) in enumerate(((ya_ref, ga_ref), (yb_ref, gb_ref), (yc_ref, gc_ref))):
        term = _sigmoid(g_ref[...]) * _dot(y_ref[...].astype(BF16), wb_ref[b])
        acc = term if acc is None else acc + term
    h = _dot(acc.astype(BF16), wo_ref[...])
    ms = jnp.mean(h * h, axis=-1, keepdims=True)
    o_ref[...] = x_ref[...] + h * lax.rsqrt(ms + EPS) * nw_ref[...]


def _merge_out(ya, yb, yc, p, off, w_branch, w_out, nw, x, tm):
    s, bw = ya.shape
    d = x.shape[1]
    gbase = off["gates"] // d
    once = pl.Buffered(1)
    yspec = pl.BlockSpec((tm, bw), lambda i: (i, 0))
    gspec = lambda b: pl.BlockSpec((tm, d), lambda i: (i, gbase + b))
    return pl.pallas_call(
        _merge_out_kernel,
        grid=(s // tm,),
        in_specs=[yspec, yspec, yspec, gspec(0), gspec(1), gspec(2),
                  pl.BlockSpec((N_BRANCH, bw, d), lambda i: (0, 0, 0), pipeline_mode=once),
                  pl.BlockSpec((d, d), lambda i: (0, 0), pipeline_mode=once),
                  pl.BlockSpec((1, d), lambda i: (0, 0)),
                  pl.BlockSpec((tm, d), lambda i: (i, 0))],
        out_specs=pl.BlockSpec((tm, d), lambda i: (i, 0)),
        out_shape=jax.ShapeDtypeStruct((s, d), F32),
        compiler_params=_params("arbitrary"),
    )(ya, yb, yc, p, p, p, w_branch, w_out, nw, x)


def _ffn_kernel(x_ref, npre_ref, npost_ref, wg_ref, wu_ref, wd_ref, o_ref, h_ref, acc_ref):
    kk = pl.program_id(1)

    @pl.when(kk == 0)
    def _():
        x = x_ref[...]
        ms = jnp.mean(x * x, axis=-1, keepdims=True)
        h_ref[...] = (x * lax.rsqrt(ms + EPS) * npre_ref[...]).astype(BF16)
        acc_ref[...] = jnp.zeros_like(acc_ref)

    h = h_ref[...]
    act = (_silu(_dot(h, wg_ref[...])) * _dot(h, wu_ref[...])).astype(BF16)
    acc_ref[...] += _dot(act, wd_ref[...])

    @pl.when(kk == pl.num_programs(1) - 1)
    def _():
        y = acc_ref[...]
        ms = jnp.mean(y * y, axis=-1, keepdims=True)
        o_ref[...] = x_ref[...] + y * lax.rsqrt(ms + EPS) * npost_ref[...]


def _ffn(x, npre, npost, wg, wu, wd, tm, tf):
    s, d = x.shape
    f = wg.shape[1]
    return pl.pallas_call(
        _ffn_kernel,
        grid=(s // tm, f // tf),
        in_specs=[pl.BlockSpec((tm, d), lambda i, k: (i, 0)),
                  pl.BlockSpec((1, d), lambda i, k: (0, 0)),
                  pl.BlockSpec((1, d), lambda i, k: (0, 0)),
                  pl.BlockSpec((d, tf), lambda i, k: (0, k)),
                  pl.BlockSpec((d, tf), lambda i, k: (0, k)),
                  pl.BlockSpec((tf, d), lambda i, k: (k, 0))],
        out_specs=pl.BlockSpec((tm, d), lambda i, k: (i, 0)),
        out_shape=jax.ShapeDtypeStruct((s, d), F32),
        scratch_shapes=[pltpu.VMEM((tm, d), BF16), pltpu.VMEM((tm, d), F32)],
        compiler_params=_params("arbitrary", "arbitrary"),
    )(x, npre, npost, wg, wu, wd)


def _split_w_in(w_in, d_model, bw):
    sizes = (bw, bw, 3 * bw, bw, GDN_HEADS, GDN_HEADS,
             3 * bw + RWKV_W_RANK + RWKV_A_RANK + RWKV_G_RANK, N_BRANCH * d_model)
    idx = [int(i) for i in np.cumsum(sizes)[:-1]]
    return jnp.split(w_in, idx, axis=-1)


def _permute_w_in(w_in, d_model, bw, total):
    lru_x, lru_g, qkv, z, beta, alpha, rw, gates = (t.astype(BF16) for t in _split_w_in(w_in, d_model, bw))
    rkv, lo = rw[:, :3 * bw], rw[:, 3 * bw:]
    used = 9 * bw + N_BRANCH * d_model + lo.shape[1] + beta.shape[1] + alpha.shape[1]
    pad = jnp.zeros((w_in.shape[0], total - used), BF16)
    return jnp.concatenate([gates, lru_x, lru_g, qkv, z, rkv, lo, beta, alpha, pad], axis=1)


def _block_diag_gates(w_a, w_x, group):
    nblk, bs, _ = w_a.shape
    ng = nblk // group
    gw = group * bs
    eye = jnp.eye(group, dtype=w_a.dtype)[None, :, None, :, None]

    def bd(w):
        return (w.reshape(ng, group, bs, 1, bs) * eye).reshape(ng, gw, gw)

    return jnp.concatenate([bd(w_a), bd(w_x)], axis=2)


def _pad_rows(w, rows, at):
    return jnp.pad(w, ((at, rows - at - w.shape[0]), (0, 0)))


def _tiles(s):
    return dict(tm_in=min(s, 1024), tn_in=512, t_lru=min(s, 256),
                t_gdn=min(s, 512), hb_gdn=4, t_rwkv=min(s, 512), hb_rwkv=4,
                tm_merge=min(s, 256), tm_ffn=min(s, 512), tf_ffn=512)


def _layer(x, lp, tiles):
    s, d = x.shape
    bw = lp["lru_conv_w"].shape[1]
    off = _layout(d, bw)
    row = lambda v: v.reshape(1, -1)

    p = _norm_matmul(x, row(lp["norm_mix_pre"]), lp["w_in_perm"], tiles["tm_in"], tiles["tn_in"])

    y_a = _lru_branch(p, off, lp["lru_conv_w"], row(lp["lru_conv_b"]), lp["lru_wax"], row(lp["lru_b_a"]),
                      row(lp["lru_b_x"]), row(lp["lru_lambda"]), tiles["t_lru"])
    y_b = _gdn_branch(p, off, lp["gdn_conv_w"], row(lp["gdn_a_log"]), row(lp["gdn_dt_bias"]),
                      row(lp["gdn_norm_w"]), tiles["t_gdn"], tiles["hb_gdn"])
    mu = lp["rwkv_mu"]
    mu_s = jnp.pad(mu[3 * bw:], (0, SMALL_W - SM_BETA)).reshape(1, SMALL_W)
    y_c = _rwkv_branch(p, off, row(mu[:bw]), row(mu[bw:2 * bw]), row(mu[2 * bw:3 * bw]), mu_s,
                       row(lp["rwkv_w0"]), row(lp["rwkv_a0"]), row(lp["rwkv_k_k"]), row(lp["rwkv_k_a"]),
                       row(lp["rwkv_r_k"]), row(lp["rwkv_ln_w"]), row(lp["rwkv_ln_b"]),
                       _pad_rows(lp["rwkv_w_up"], LANES, SM_WLO), _pad_rows(lp["rwkv_a_up"], LANES, SM_ALO),
                       _pad_rows(lp["rwkv_g_up"], SM_GLO_PAD, 0), tiles["t_rwkv"], tiles["hb_rwkv"])

    x = _merge_out(y_a, y_b, y_c, p, off, lp["w_branch"], lp["w_out"], row(lp["norm_mix_post"]), x,
                   tiles["tm_merge"])
    x = _ffn(x, row(lp["norm_ffn_pre"]), row(lp["norm_ffn_post"]), lp["ffn_w_gate"], lp["ffn_w_up"],
             lp["ffn_w_down"], tiles["tm_ffn"], tiles["tf_ffn"])
    return x


def kernel(x, norm_mix_pre, norm_mix_post, norm_ffn_pre, norm_ffn_post, w_in, lru_conv_w, lru_conv_b, lru_w_a, lru_b_a, lru_w_x, lru_b_x, lru_lambda, gdn_conv_w, gdn_a_log, gdn_dt_bias, gdn_norm_w, rwkv_mu, rwkv_w0, rwkv_w_up, rwkv_a0, rwkv_a_up, rwkv_g_up, rwkv_k_k, rwkv_k_a, rwkv_r_k, rwkv_ln_w, rwkv_ln_b, w_branch, w_out, ffn_w_gate, ffn_w_up, ffn_w_down):
    batch, s, d = x.shape
    depth = w_in.shape[0]
    bw = lru_conv_w.shape[2]
    total = _layout(d, bw)["total"]
    tiles = _tiles(s)
    outs = []
    for b in range(batch):
        xb = x.reshape(s, d) if batch == 1 else x[b]
        for l in range(depth):
            lp = dict(
                norm_mix_pre=norm_mix_pre[l], norm_mix_post=norm_mix_post[l],
                norm_ffn_pre=norm_ffn_pre[l], norm_ffn_post=norm_ffn_post[l],
                w_in_perm=_permute_w_in(w_in[l], d, bw, total),
                lru_conv_w=lru_conv_w[l], lru_conv_b=lru_conv_b[l],
                lru_wax=_block_diag_gates(lru_w_a[l], lru_w_x[l], 2 * LANES // LRU_BLOCK),
                lru_b_a=lru_b_a[l], lru_b_x=lru_b_x[l], lru_lambda=lru_lambda[l],
                gdn_conv_w=gdn_conv_w[l], gdn_a_log=gdn_a_log[l], gdn_dt_bias=gdn_dt_bias[l],
                gdn_norm_w=gdn_norm_w[l],
                rwkv_mu=rwkv_mu[l], rwkv_w0=rwkv_w0[l], rwkv_w_up=rwkv_w_up[l], rwkv_a0=rwkv_a0[l],
                rwkv_a_up=rwkv_a_up[l], rwkv_g_up=rwkv_g_up[l], rwkv_k_k=rwkv_k_k[l], rwkv_k_a=rwkv_k_a[l],
                rwkv_r_k=rwkv_r_k[l], rwkv_ln_w=rwkv_ln_w[l], rwkv_ln_b=rwkv_ln_b[l],
                w_branch=w_branch[l].astype(BF16), w_out=w_out[l].astype(BF16),
                ffn_w_gate=ffn_w_gate[l].astype(BF16), ffn_w_up=ffn_w_up[l].astype(BF16),
                ffn_w_down=ffn_w_down[l].astype(BF16))
            xb = _layer(xb, lp, tiles)
        outs.append(xb)
    return outs[0].reshape(1, s, d) if batch == 1 else jnp.stack(outs, axis=0)
```

```python
import functools

import jax
import jax.numpy as jnp
import numpy as np
from jax import lax
from jax.experimental import pallas as pl
from jax.experimental.pallas import tpu as pltpu

F32 = jnp.float32
BF16 = jnp.bfloat16

EPS = 1e-6
LRU_C = 8.0
LRU_BLOCK = 64
GDN_HEADS = 8
GDN_HEAD_DIM = 128
RWKV_HEAD = 64
RWKV_W_RANK = 64
RWKV_A_RANK = 64
RWKV_G_RANK = 160
RWKV_GN_EPS = 64e-5
N_BRANCH = 3
CONV_W = 4

CHUNK = 64
LANES = 128
SUBLANES = 8
VMEM_LIMIT = 56 * 1024 * 1024

SMALL_W = 512


def _layout(d_model, branch_w):
    off = {}
    c = 0
    for name, width in (("gates", N_BRANCH * d_model), ("lru_x", branch_w), ("lru_g", branch_w),
                        ("gdn_q", branch_w), ("gdn_k", branch_w), ("gdn_v", branch_w), ("gdn_z", branch_w),
                        ("rw_r", branch_w), ("rw_k", branch_w), ("rw_v", branch_w), ("small", SMALL_W)):
        off[name] = c
        c += width
    off["total"] = c
    return off


SM_WLO = 0
SM_ALO = SM_WLO + RWKV_W_RANK
SM_GLO = SM_ALO + RWKV_A_RANK
SM_BETA = SM_GLO + RWKV_G_RANK
SM_ALPHA = SM_BETA + GDN_HEADS
SM_USED = SM_ALPHA + GDN_HEADS
SM_GLO_PAD = 256
SM_PAR = (SM_BETA // LANES) * LANES
assert SM_PAR <= SM_BETA and SM_USED <= SM_PAR + LANES


def _params(*sem):
    return pltpu.CompilerParams(dimension_semantics=sem, vmem_limit_bytes=VMEM_LIMIT)


def _dot(a, b):
    return jnp.dot(a, b, preferred_element_type=F32)


def _dot_nt(a, b):
    return lax.dot_general(a, b, (((1,), (1,)), ((), ())), preferred_element_type=F32)


def _dot_tn(a, b):
    return lax.dot_general(a, b, (((0,), (0,)), ((), ())), preferred_element_type=F32)


def _softplus(x):
    return jnp.maximum(x, 0.0) + jnp.log1p(jnp.exp(-jnp.abs(x)))


def _sigmoid(x):
    return 1.0 / (1.0 + jnp.exp(-x))


def _silu(x):
    return x * _sigmoid(x)


def _gelu_tanh(x):
    c = np.float32(np.sqrt(2.0 / np.pi))
    return 0.5 * x * (1.0 + jnp.tanh(c * (x + 0.044715 * (x * x * x))))


def _tri_masks(n):
    row = lax.broadcasted_iota(jnp.int32, (n, n), 0)
    col = lax.broadcasted_iota(jnp.int32, (n, n), 1)
    return row >= col, row > col, row == col


def _neumann_solve(nils, xs):
    steps = int(np.log2(CHUNK))
    for it in range(steps):
        xs = [x + _dot(p, x) for p, x in zip(nils, xs)]
        if it + 1 < steps:
            nils = [_dot(p, p) for p in nils]
    return xs


def _seg_cumsum(x, seg):
    row = lax.broadcasted_iota(jnp.int32, x.shape, 0)
    pos = row & (seg - 1)
    k = 1
    while k < seg:
        x = jnp.where(pos >= k, x + pltpu.roll(x, k, axis=0), x)
        k *= 2
    return x


def _stage_rows(buf_ref, x):
    buf_ref[SUBLANES:, :] = x


def _rows_before(buf_ref, t, k):
    return buf_ref[pl.ds(SUBLANES - k, t), :]


def _keep_tail(buf_ref, t):
    buf_ref[0:SUBLANES, :] = buf_ref[pl.ds(t, SUBLANES), :]


def _zero_tail(buf_ref):
    buf_ref[0:SUBLANES, :] = jnp.zeros((SUBLANES, buf_ref.shape[1]), buf_ref.dtype)


def _norm_matmul_kernel(x_ref, nw_ref, w_ref, o_ref, h_ref):
    @pl.when(pl.program_id(1) == 0)
    def _():
        x = x_ref[...]
        ms = jnp.mean(x * x, axis=-1, keepdims=True)
        h_ref[...] = (x * lax.rsqrt(ms + EPS) * nw_ref[...]).astype(BF16)

    o_ref[...] = _dot(h_ref[...], w_ref[...])


def _norm_matmul(x, nw, w, layer, tm, tn):
    s, d = x.shape
    n = w.shape[2]
    return pl.pallas_call(
        _norm_matmul_kernel,
        grid=(s // tm, n // tn),
        in_specs=[pl.BlockSpec((tm, d), lambda i, j: (i, 0)),
                  pl.BlockSpec((1, d), lambda i, j: (0, 0)),
                  pl.BlockSpec((None, d, tn), lambda i, j: (layer, 0, j))],
        out_specs=pl.BlockSpec((tm, tn), lambda i, j: (i, j)),
        out_shape=jax.ShapeDtypeStruct((s, n), F32),
        scratch_shapes=[pltpu.VMEM((tm, d), BF16)],
        compiler_params=_params("arbitrary", "arbitrary"),
    )(x, nw, w)


def _lru_kernel(x_ref, g_ref, cw_ref, cb_ref, wax_ref, ba_ref, bx_ref, lam_ref, o_ref, tail_ref, hc_ref):
    t_blk = x_ref.shape[0]
    width = x_ref.shape[1]

    @pl.when(pl.program_id(0) == 0)
    def _():
        _zero_tail(tail_ref)
        hc_ref[...] = jnp.zeros_like(hc_ref)

    x = x_ref[...]
    cw = cw_ref[...]
    _stage_rows(tail_ref, x)
    xc = x * cw[3:4] + cb_ref[...]
    for k in range(1, CONV_W):
        xc = xc + _rows_before(tail_ref, t_blk, k) * cw[CONV_W - 1 - k:CONV_W - k]
    _keep_tail(tail_ref, t_blk)

    nb = wax_ref.shape[0]
    gw = width // nb
    r_parts, i_parts = [], []
    for b in range(nb):
        ri = _dot(xc[:, gw * b:gw * (b + 1)], wax_ref[b])
        r_parts.append(ri[:, :gw])
        i_parts.append(ri[:, gw:])
    r = _sigmoid(jnp.concatenate(r_parts, axis=1) + ba_ref[...])
    i = _sigmoid(jnp.concatenate(i_parts, axis=1) + bx_ref[...])
    log_a = (-LRU_C) * r * _softplus(-lam_ref[...])
    a = jnp.exp(log_a)
    u = jnp.sqrt(1.0 - jnp.exp(2.0 * log_a)) * (i * xc)

    row = lax.broadcasted_iota(jnp.int32, a.shape, 0)
    k = 1
    while k < t_blk:
        m = row >= k
        u = jnp.where(m, a * pltpu.roll(u, k, axis=0) + u, u)
        a = jnp.where(m, a * pltpu.roll(a, k, axis=0), a)
        k *= 2
    h = u + a * hc_ref[0:1, :]
    hc_ref[...] = jnp.broadcast_to(h[t_blk - 1:t_blk, :], hc_ref.shape)
    o_ref[...] = _gelu_tanh(g_ref[...]) * h


def _lru_branch(p, off, cw, cb, wax, ba, bx, lam, t_blk):
    s = p.shape[0]
    width = cw.shape[1]
    xb = off["lru_x"] // width
    gb = off["lru_g"] // width
    vec = pl.BlockSpec((1, width), lambda t: (0, 0))
    return pl.pallas_call(
        _lru_kernel,
        grid=(s // t_blk,),
        in_specs=[pl.BlockSpec((t_blk, width), lambda t: (t, xb)),
                  pl.BlockSpec((t_blk, width), lambda t: (t, gb)),
                  pl.BlockSpec((CONV_W, width), lambda t: (0, 0)),
                  vec,
                  pl.BlockSpec(wax.shape, lambda t: (0, 0, 0)),
                  vec, vec, vec],
        out_specs=pl.BlockSpec((t_blk, width), lambda t: (t, 0)),
        out_shape=jax.ShapeDtypeStruct((s, width), F32),
        scratch_shapes=[pltpu.VMEM((t_blk + SUBLANES, width), F32), pltpu.VMEM((SUBLANES, width), F32)],
        compiler_params=_params("arbitrary"),
    )(p, p, cw, cb, wax, ba, bx, lam)


def _gdn_kernel(q_ref, k_ref, v_ref, z_ref, sm_ref, cwq_ref, cwk_ref, cwv_ref, alog_ref, dtb_ref, nw_ref,
                o_ref, tq_ref, tk_ref, tv_ref, st_ref):
    t_blk = q_ref.shape[0]
    hd = GDN_HEAD_DIM
    nh = q_ref.shape[1] // hd
    n_chunks = t_blk // CHUNK
    head0 = pl.program_id(0) * nh

    @pl.when(pl.program_id(1) == 0)
    def _():
        _zero_tail(tq_ref)
        _zero_tail(tk_ref)
        _zero_tail(tv_ref)
        st_ref[...] = jnp.zeros_like(st_ref)

    def conv_silu(x_ref, tail_ref, cw_ref):
        x = x_ref[...]
        cw = cw_ref[...]
        _stage_rows(tail_ref, x)
        y = x * cw[3:4]
        for k in range(1, CONV_W):
            y = y + _rows_before(tail_ref, t_blk, k) * cw[CONV_W - 1 - k:CONV_W - k]
        _keep_tail(tail_ref, t_blk)
        return _silu(y)

    q_all = conv_silu(q_ref, tq_ref, cwq_ref)
    k_all = conv_silu(k_ref, tk_ref, cwk_ref)
    v_all = conv_silu(v_ref, tv_ref, cwv_ref)

    par = sm_ref[...]
    lane = lax.broadcasted_iota(jnp.int32, (1, LANES), 1)
    beta_all = _sigmoid(par)
    gc_all = _seg_cumsum(-jnp.exp(alog_ref[...]) * _softplus(par + dtb_ref[...]), CHUNK)
    causal, strict, eye = _tri_masks(CHUNK)

    qs, ks, vs, betas, gcs, egcs = [], [], [], [], [], []
    for h in range(nh):
        hs = slice(h * hd, (h + 1) * hd)
        q = q_all[:, hs]
        k = k_all[:, hs]
        qs.append(q * lax.rsqrt(jnp.sum(q * q, axis=-1, keepdims=True) + EPS) * (hd ** -0.5))
        ks.append(k * lax.rsqrt(jnp.sum(k * k, axis=-1, keepdims=True) + EPS))
        vs.append(v_all[:, hs])
        beta = jnp.sum(jnp.where(lane == SM_BETA - SM_PAR + head0 + h, beta_all, 0.0), axis=1, keepdims=True)
        gc = jnp.sum(jnp.where(lane == SM_ALPHA - SM_PAR + head0 + h, gc_all, 0.0), axis=1, keepdims=True)
        betas.append(jnp.broadcast_to(beta, (t_blk, hd)))
        gc = jnp.broadcast_to(gc, (t_blk, hd))
        gcs.append(gc)
        egcs.append(jnp.exp(gc))

    items = [(h, c) for h in range(nh) for c in range(n_chunks)]

    def rows(arrs, h, c):
        return arrs[h][c * CHUNK:(c + 1) * CHUNK]

    kq = [_dot_nt(jnp.concatenate([rows(ks, h, c), rows(qs, h, c)], axis=0), rows(ks, h, c)) for h, c in items]
    nil, attn, xs = [], [], []
    for (h, c), m in zip(items, kq):
        gcc, bc, kc = rows(gcs, h, c), rows(betas, h, c), rows(ks, h, c)
        g_col = gcc[:, :CHUNK]
        g_row = jnp.sum(jnp.where(eye, g_col, 0.0), axis=0, keepdims=True)
        decay = jnp.exp(jnp.where(causal, g_col - g_row, 0.0))
        nil.append(jnp.where(strict, -(bc[:, :CHUNK] * m[:CHUNK] * decay), 0.0))
        attn.append(jnp.where(causal, m[CHUNK:] * decay, 0.0))
        xs.append(jnp.concatenate([bc * rows(vs, h, c), bc * rows(egcs, h, c) * kc], axis=1))
    xs = _neumann_solve(nil, xs)
    g_last = [rows(gcs, h, c)[CHUNK - 1:CHUNK, :] for h, c in items]
    tn = [_dot_tn(rows(ks, h, c) * jnp.exp(gl - rows(gcs, h, c)), x) for (h, c), gl, x in zip(items, g_last, xs)]
    ax = [_dot(a_, x) for a_, x in zip(attn, xs)]
    q_eff = [rows(qs, h, c) * rows(egcs, h, c) - a_[:, hd:] for (h, c), a_ in zip(items, ax)]
    lhs = [jnp.concatenate([t_[:, hd:], q_], axis=0) for t_, q_ in zip(tn, q_eff)]
    e_last = [jnp.exp(gl) for gl in g_last]

    states = [st_ref[h] for h in range(nh)]
    outs = [[] for _ in range(nh)]
    for c in range(n_chunks):
        res = [_dot(lhs[h * n_chunks + c], states[h]) for h in range(nh)]
        for h in range(nh):
            i = h * n_chunks + c
            states[h] = states[h] * e_last[i] - res[h][:hd] + tn[i][:, :hd]
            outs[h].append(res[h][hd:] + ax[i][:, :hd])
    for h in range(nh):
        st_ref[h] = states[h]

    z = z_ref[...]
    for h in range(nh):
        o = jnp.concatenate(outs[h], axis=0)
        o = o * lax.rsqrt(jnp.mean(o * o, axis=-1, keepdims=True) + EPS) * nw_ref[...]
        o_ref[:, h * hd:(h + 1) * hd] = o * _silu(z[:, h * hd:(h + 1) * hd])


def _gdn_branch(p, off, conv_w, a_log, dt_bias, norm_w, t_blk, heads_per_step):
    s = p.shape[0]
    hd = GDN_HEAD_DIM
    nh = GDN_HEADS
    bwid = heads_per_step * hd
    ngrp = nh // heads_per_step
    qb, kb, vb, zb = (off[n] // bwid for n in ("gdn_q", "gdn_k", "gdn_v", "gdn_z"))
    parb = (off["small"] + SM_PAR) // LANES
    blk = lambda base: pl.BlockSpec((t_blk, bwid), lambda h, t: (t, base + h))
    cwb = lambda base: pl.BlockSpec((CONV_W, bwid), lambda h, t: (0, base + h))
    lane_row = pl.BlockSpec((1, LANES), lambda h, t: (0, 0))
    at_alpha = lambda v: jnp.pad(v, ((0, 0), (SM_ALPHA - SM_PAR, LANES - (SM_ALPHA - SM_PAR) - nh)))
    return pl.pallas_call(
        _gdn_kernel,
        grid=(ngrp, s // t_blk),
        in_specs=[blk(qb), blk(kb), blk(vb), blk(zb),
                  pl.BlockSpec((t_blk, LANES), lambda h, t: (t, parb)),
                  cwb(0), cwb(ngrp), cwb(2 * ngrp),
                  lane_row, lane_row,
                  pl.BlockSpec((1, hd), lambda h, t: (0, 0))],
        out_specs=pl.BlockSpec((t_blk, bwid), lambda h, t: (t, h)),
        out_shape=jax.ShapeDtypeStruct((s, nh * hd), F32),
        scratch_shapes=([pltpu.VMEM((t_blk + SUBLANES, bwid), F32)] * 3
                        + [pltpu.VMEM((heads_per_step, hd, hd), F32)]),
        compiler_params=_params("arbitrary", "arbitrary"),
    )(p, p, p, p, p, conv_w, conv_w, conv_w, at_alpha(a_log), at_alpha(dt_bias), norm_w)


def _rwkv_kernel(r_ref, k_ref, v_ref, sm_ref, mur_ref, muk_ref, muv_ref, mus_ref, w0_ref, a0_ref, kk_ref, ka_ref,
                 rk_ref, lnw_ref, lnb_ref, wup_ref, aup_ref, gup_ref,
                 o_ref, tr_ref, tk_ref, tv_ref, ts_ref, st_ref):
    t_blk, pw = r_ref.shape
    hd = RWKV_HEAD
    nh = pw // hd

    @pl.when(pl.program_id(1) == 0)
    def _():
        _zero_tail(tr_ref)
        _zero_tail(tk_ref)
        _zero_tail(tv_ref)
        _zero_tail(ts_ref)
        st_ref[...] = jnp.zeros_like(st_ref)

    def token_shift(x_ref, tail_ref, mu_ref):
        x = x_ref[...]
        _stage_rows(tail_ref, x)
        prev = _rows_before(tail_ref, t_blk, 1)
        _keep_tail(tail_ref, t_blk)
        return x + (prev - x) * mu_ref[...]

    r = token_shift(r_ref, tr_ref, mur_ref)
    k = token_shift(k_ref, tk_ref, muk_ref)
    v = token_shift(v_ref, tv_ref, muv_ref)
    sm = token_shift(sm_ref, ts_ref, mus_ref)

    lo = sm[:, SM_WLO:SM_WLO + LANES]
    w = -_softplus(-(w0_ref[...] + _dot(jnp.tanh(lo), wup_ref[...]))) - 0.5
    a = _sigmoid(a0_ref[...] + _dot(lo, aup_ref[...]))
    g = _dot(_sigmoid(sm[:, SM_GLO:SM_GLO + SM_GLO_PAD]), gup_ref[...])
    log_dec = -jnp.exp(w)

    same_head = (lax.broadcasted_iota(jnp.int32, (pw, pw), 0) // hd) == (lax.broadcasted_iota(jnp.int32, (pw, pw), 1) // hd)
    ones_bd = jnp.where(same_head, 1.0, 0.0).astype(BF16)

    def head_sum(x):
        hi = x.astype(BF16)
        lo = (x - hi.astype(F32)).astype(BF16)
        return _dot(hi, ones_bd) + _dot(lo, ones_bd)

    kraw = k * kk_ref[...]
    kk = kraw * lax.rsqrt(head_sum(kraw * kraw) + EPS)
    k2 = k * (1.0 + (a - 1.0) * ka_ref[...])

    gcs = _seg_cumsum(log_dec, CHUNK)
    e_inc = jnp.exp(gcs)
    e_exc = jnp.exp(gcs - log_dec)
    e_neg = jnp.exp(-gcs)
    xa = -kk * e_exc
    xr = r * e_inc
    yb = kk * a * e_neg
    yk = k2 * e_neg

    n_chunks = t_blk // CHUNK
    npair = pw // LANES
    c2 = 2 * CHUNK
    lane1 = lax.broadcasted_iota(jnp.int32, (1, LANES), 1)
    own = [lane1 < hd, lane1 >= hd]
    causal, strict, _ = _tri_masks(CHUNK)
    r2 = lax.broadcasted_iota(jnp.int32, (c2, c2), 0)
    j2 = lax.broadcasted_iota(jnp.int32, (c2, c2), 1)
    mask2 = (r2 & (CHUNK - 1)) - (j2 & (CHUNK - 1)) + jnp.where(r2 < CHUNK, 0, 1) > 0
    eye2 = r2 == j2
    zeros_c = jnp.zeros((CHUNK, LANES), F32)

    def grp(arr, pp, c):
        return arr[c * CHUNK:(c + 1) * CHUNK, pp * LANES:(pp + 1) * LANES]

    v_sw = [pltpu.roll(v[:, pp * LANES:(pp + 1) * LANES], hd, axis=1) for pp in range(npair)]
    groups = [(pp, c) for pp in range(npair) for c in range(n_chunks)]
    items = [(pp, c, hh) for pp, c in groups for hh in range(2)]

    xcat = {g_: jnp.concatenate([grp(xa, *g_), grp(xr, *g_)], axis=0) for g_ in groups}
    ycat = {g_: jnp.concatenate([grp(yb, *g_), grp(yk, *g_)], axis=0) for g_ in groups}
    d_row = {(pp, c): e_inc[(c + 1) * CHUNK - 1:(c + 1) * CHUNK, pp * LANES:(pp + 1) * LANES] for pp, c in groups}
    ms = [_dot_nt(xcat[(pp, c)], jnp.where(own[hh], ycat[(pp, c)], 0.0)) for pp, c, hh in items]
    y_t = [jnp.where(own[hh], ycat[(pp, c)] * d_row[(pp, c)], 0.0).T for pp, c, hh in items]
    nil = [jnp.where(strict, m[:CHUNK, :CHUNK], 0.0) for m in ms]
    a_rb = [jnp.where(causal, m[CHUNK:, :CHUNK], 0.0) for m in ms]
    v_oth = [jnp.where(own[hh], 0.0, v_sw[pp][c * CHUNK:(c + 1) * CHUNK]) for pp, c, hh in items]
    avk = [_dot(jnp.where(mask2, m, 0.0), jnp.concatenate([zeros_c, vo], axis=0)) for m, vo in zip(ms, v_oth)]
    xs = [jnp.where(own[hh], grp(xa, pp, c), av[:CHUNK]) for av, (pp, c, hh) in zip(avk, items)]
    xs = _neumann_solve(nil, xs)
    tt = [_dot(yt, jnp.concatenate([x, vo], axis=0)) for yt, x, vo in zip(y_t, xs, v_oth)]
    rx = [_dot(a_, x) for a_, x in zip(a_rb, xs)]
    q_eff = [jnp.where(own[hh], grp(xr, pp, c) + r_, 0.0) for r_, (pp, c, hh) in zip(rx, items)]
    o_c = [r_ + av[CHUNK:] for r_, av in zip(rx, avk)]
    lhs, h_c, o_cc, d_col = {}, {}, {}, {}
    for gi, g_ in enumerate(groups):
        i0, i1 = 2 * gi, 2 * gi + 1
        lhs[g_] = jnp.concatenate([jnp.where(own[0], tt[i0], tt[i1]), q_eff[i0], q_eff[i1]], axis=0)
        h_c[g_] = jnp.where(own[0], tt[i1], tt[i0])
        o_cc[g_] = jnp.where(own[0], o_c[i1], o_c[i0])
        d_col[g_] = jnp.sum(jnp.where(eye2, d_row[g_], 0.0), axis=1, keepdims=True)

    states = [st_ref[pp] for pp in range(npair)]
    outs = [[] for _ in range(npair)]
    for c in range(n_chunks):
        res = [_dot(lhs[(pp, c)], states[pp]) for pp in range(npair)]
        for pp in range(npair):
            states[pp] = d_col[(pp, c)] * states[pp] + res[pp][:c2] + h_c[(pp, c)]
            outs[pp].append(res[pp][c2:c2 + CHUNK] + res[pp][c2 + CHUNK:] + o_cc[(pp, c)])
    for pp in range(npair):
        st_ref[pp] = states[pp]
    out = jnp.concatenate([pltpu.roll(jnp.concatenate(o, axis=0), hd, axis=1) for o in outs], axis=1)

    inv_n = 1.0 / hd
    mean = head_sum(out) * inv_n
    cen = out - mean
    var = head_sum(cen * cen) * inv_n
    gn = cen * lax.rsqrt(var + RWKV_GN_EPS) * lnw_ref[...] + lnb_ref[...]
    bonus = head_sum(r * k2 * rk_ref[...]) * v
    o_ref[...] = (gn + bonus) * g


def _rwkv_branch(p, off, mu_r, mu_k, mu_v, mu_s, w0, a0, k_k, k_a, r_k, ln_w, ln_b, w_up, a_up, g_up, t_blk,
                 heads_per_step):
    s = p.shape[0]
    pw = heads_per_step * RWKV_HEAD
    width = w0.shape[1]
    ngrp = width // pw
    rb, kb, vb = (off[n] // pw for n in ("rw_r", "rw_k", "rw_v"))
    smb = off["small"] // SMALL_W
    blk = lambda base: pl.BlockSpec((t_blk, pw), lambda h, t: (t, base + h))
    vec = pl.BlockSpec((1, pw), lambda h, t: (0, h))
    return pl.pallas_call(
        _rwkv_kernel,
        grid=(ngrp, s // t_blk),
        in_specs=[blk(rb), blk(kb), blk(vb),
                  pl.BlockSpec((t_blk, SMALL_W), lambda h, t: (t, smb)),
                  vec, vec, vec,
                  pl.BlockSpec((1, SMALL_W), lambda h, t: (0, 0)),
                  vec, vec, vec, vec, vec, vec, vec,
                  pl.BlockSpec((LANES, pw), lambda h, t: (0, h)),
                  pl.BlockSpec((LANES, pw), lambda h, t: (0, h)),
                  pl.BlockSpec((SM_GLO_PAD, pw), lambda h, t: (0, h))],
        out_specs=pl.BlockSpec((t_blk, pw), lambda h, t: (t, h)),
        out_shape=jax.ShapeDtypeStruct((s, width), F32),
        scratch_shapes=([pltpu.VMEM((t_blk + SUBLANES, pw), F32)] * 3
                        + [pltpu.VMEM((t_blk + SUBLANES, SMALL_W), F32),
                           pltpu.VMEM((pw // LANES, LANES, LANES), F32)]),
        compiler_params=_params("arbitrary", "arbitrary"),
    )(p, p, p, p, mu_r, mu_k, mu_v, mu_s, w0, a0, k_k, k_a, r_k, ln_w, ln_b, w_up, a_up, g_up)


def _merge_out_kernel(ya_ref, yb_ref, yc_ref, ga_ref, gb_ref, gc_ref, wb_ref, wo_ref, nw_ref, x_ref, o_ref):
    acc = None
    for b, (y_ref, g_ref) in enumerate(((ya_ref, ga_ref), (yb_ref, gb_ref), (yc_ref, gc_ref))):
        term = _sigmoid(g_ref[...]) * _dot(y_ref[...].astype(BF16), wb_ref[b])
        acc = term if acc is None else acc + term
    h = _dot(acc.astype(BF16), wo_ref[...])
    ms = jnp.mean(h * h, axis=-1, keepdims=True)
    o_ref[...] = x_ref[...] + h * lax.rsqrt(ms + EPS) * nw_ref[...]


def _merge_out(ya, yb, yc, p, off, w_branch, w_out, layer, nw, x, tm):
    s, bw = ya.shape
    d = x.shape[1]
    gbase = off["gates"] // d
    once = pl.Buffered(1)
    yspec = pl.BlockSpec((tm, bw), lambda i: (i, 0))
    gspec = lambda b: pl.BlockSpec((tm, d), lambda i: (i, gbase + b))
    return pl.pallas_call(
        _merge_out_kernel,
        grid=(s // tm,),
        in_specs=[yspec, yspec, yspec, gspec(0), gspec(1), gspec(2),
                  pl.BlockSpec((None, N_BRANCH, bw, d), lambda i: (layer, 0, 0, 0), pipeline_mode=once),
                  pl.BlockSpec((None, d, d), lambda i: (layer, 0, 0), pipeline_mode=once),
                  pl.BlockSpec((1, d), lambda i: (0, 0)),
                  pl.BlockSpec((tm, d), lambda i: (i, 0))],
        out_specs=pl.BlockSpec((tm, d), lambda i: (i, 0)),
        out_shape=jax.ShapeDtypeStruct((s, d), F32),
        compiler_params=_params("arbitrary"),
    )(ya, yb, yc, p, p, p, w_branch, w_out, nw, x)


def _ffn_kernel(x_ref, npre_ref, npost_ref, wg_ref, wu_ref, wd_ref, o_ref, h_ref, acc_ref):
    kk = pl.program_id(1)

    @pl.when(kk == 0)
    def _():
        x = x_ref[...]
        ms = jnp.mean(x * x, axis=-1, keepdims=True)
        h_ref[...] = (x * lax.rsqrt(ms + EPS) * npre_ref[...]).astype(BF16)
        acc_ref[...] = jnp.zeros_like(acc_ref)

    h = h_ref[...]
    act = (_silu(_dot(h, wg_ref[...])) * _dot(h, wu_ref[...])).astype(BF16)
    acc_ref[...] += _dot(act, wd_ref[...])

    @pl.when(kk == pl.num_programs(1) - 1)
    def _():
        y = acc_ref[...]
        ms = jnp.mean(y * y, axis=-1, keepdims=True)
        o_ref[...] = x_ref[...] + y * lax.rsqrt(ms + EPS) * npost_ref[...]


def _ffn(x, npre, npost, wg, wu, wd, layer, tm, tf):
    s, d = x.shape
    f = wg.shape[2]
    return pl.pallas_call(
        _ffn_kernel,
        grid=(s // tm, f // tf),
        in_specs=[pl.BlockSpec((tm, d), lambda i, k: (i, 0)),
                  pl.BlockSpec((1, d), lambda i, k: (0, 0)),
                  pl.BlockSpec((1, d), lambda i, k: (0, 0)),
                  pl.BlockSpec((None, d, tf), lambda i, k: (layer, 0, k)),
                  pl.BlockSpec((None, d, tf), lambda i, k: (layer, 0, k)),
                  pl.BlockSpec((None, tf, d), lambda i, k: (layer, k, 0))],
        out_specs=pl.BlockSpec((tm, d), lambda i, k: (i, 0)),
        out_shape=jax.ShapeDtypeStruct((s, d), F32),
        scratch_shapes=[pltpu.VMEM((tm, d), BF16), pltpu.VMEM((tm, d), F32)],
        compiler_params=_params("arbitrary", "arbitrary"),
    )(x, npre, npost, wg, wu, wd)


PERMUTE_ROWS = 128
PERMUTE_CHUNK = 1024


def _permute_kernel(plan, n_in, w_ref, sm_ref, o_ref):
    for dst, src, width in plan:
        for c0 in range(0, width, PERMUTE_CHUNK):
            cw = min(PERMUTE_CHUNK, width - c0)
            s0 = src + c0
            a0 = (s0 // LANES) * LANES
            sh = s0 - a0
            lw = min(-(-(sh + cw) // LANES) * LANES, n_in - a0)
            blk = w_ref[:, a0:a0 + lw]
            o_ref[:, dst + c0:dst + c0 + cw] = blk[:, sh:sh + cw].astype(BF16)
    o_ref[:, o_ref.shape[1] - SMALL_W:] = sm_ref[...]


def _permute_w_in(w_in, d_model, bw, total):
    depth, d, n_in = w_in.shape
    off = _layout(d_model, bw)
    n_aligned = 6 * bw
    src_rw = n_aligned + 2 * GDN_HEADS
    src_lo = src_rw + 3 * bw
    src_gates = src_lo + RWKV_W_RANK + RWKV_A_RANK + RWKV_G_RANK
    plan = ((off["gates"], src_gates, N_BRANCH * d_model),
            (off["lru_x"], 0, n_aligned),
            (off["rw_r"], src_rw, 3 * bw))
    assert src_gates + N_BRANCH * d_model == n_in and off["small"] + SMALL_W == total
    small = jnp.concatenate([w_in[..., src_lo:src_gates], w_in[..., n_aligned:src_rw],
                             jnp.zeros((depth, d, SMALL_W - SM_USED), w_in.dtype)], axis=-1).astype(BF16)
    return pl.pallas_call(
        functools.partial(_permute_kernel, plan, n_in),
        grid=(depth, d // PERMUTE_ROWS),
        in_specs=[pl.BlockSpec((None, PERMUTE_ROWS, n_in), lambda l, i: (l, i, 0)),
                  pl.BlockSpec((None, PERMUTE_ROWS, SMALL_W), lambda l, i: (l, i, 0))],
        out_specs=pl.BlockSpec((None, PERMUTE_ROWS, total), lambda l, i: (l, i, 0)),
        out_shape=jax.ShapeDtypeStruct((depth, d, total), BF16),
        compiler_params=_params("arbitrary", "arbitrary"),
    )(w_in, small)


def _block_diag_gates(w_a, w_x, group):
    nblk, bs, _ = w_a.shape
    ng = nblk // group
    gw = group * bs
    eye = jnp.eye(group, dtype=w_a.dtype)[None, :, None, :, None]

    def bd(w):
        return (w.reshape(ng, group, bs, 1, bs) * eye).reshape(ng, gw, gw)

    return jnp.concatenate([bd(w_a), bd(w_x)], axis=2)


def _pad_rows(w, rows, at):
    return jnp.pad(w, ((at, rows - at - w.shape[0]), (0, 0)))


def _tiles(s):
    return dict(tm_in=min(s, 1024), tn_in=512, t_lru=min(s, 256),
                t_gdn=min(s, 512), hb_gdn=4, t_rwkv=min(s, 512), hb_rwkv=4,
                tm_merge=min(s, 256), tm_ffn=min(s, 512), tf_ffn=512)


def _layer(x, lp, big, layer, tiles):
    s, d = x.shape
    bw = lp["lru_conv_w"].shape[1]
    off = _layout(d, bw)
    row = lambda v: v.reshape(1, -1)

    p = _norm_matmul(x, row(lp["norm_mix_pre"]), big["w_in_perm"], layer, tiles["tm_in"], tiles["tn_in"])

    y_a = _lru_branch(p, off, lp["lru_conv_w"], row(lp["lru_conv_b"]), lp["lru_wax"], row(lp["lru_b_a"]),
                      row(lp["lru_b_x"]), row(lp["lru_lambda"]), tiles["t_lru"])
    y_b = _gdn_branch(p, off, lp["gdn_conv_w"], row(lp["gdn_a_log"]), row(lp["gdn_dt_bias"]),
                      row(lp["gdn_norm_w"]), tiles["t_gdn"], tiles["hb_gdn"])
    mu = lp["rwkv_mu"]
    mu_s = jnp.pad(mu[3 * bw:], (0, SMALL_W - SM_BETA)).reshape(1, SMALL_W)
    y_c = _rwkv_branch(p, off, row(mu[:bw]), row(mu[bw:2 * bw]), row(mu[2 * bw:3 * bw]), mu_s,
                       row(lp["rwkv_w0"]), row(lp["rwkv_a0"]), row(lp["rwkv_k_k"]), row(lp["rwkv_k_a"]),
                       row(lp["rwkv_r_k"]), row(lp["rwkv_ln_w"]), row(lp["rwkv_ln_b"]),
                       _pad_rows(lp["rwkv_w_up"], LANES, SM_WLO), _pad_rows(lp["rwkv_a_up"], LANES, SM_ALO),
                       _pad_rows(lp["rwkv_g_up"], SM_GLO_PAD, 0), tiles["t_rwkv"], tiles["hb_rwkv"])

    x = _merge_out(y_a, y_b, y_c, p, off, big["w_branch"], big["w_out"], layer, row(lp["norm_mix_post"]), x,
                   tiles["tm_merge"])
    x = _ffn(x, row(lp["norm_ffn_pre"]), row(lp["norm_ffn_post"]), big["ffn_w_gate"], big["ffn_w_up"],
             big["ffn_w_down"], layer, tiles["tm_ffn"], tiles["tf_ffn"])
    return x


def kernel(x, norm_mix_pre, norm_mix_post, norm_ffn_pre, norm_ffn_post, w_in, lru_conv_w, lru_conv_b, lru_w_a, lru_b_a, lru_w_x, lru_b_x, lru_lambda, gdn_conv_w, gdn_a_log, gdn_dt_bias, gdn_norm_w, rwkv_mu, rwkv_w0, rwkv_w_up, rwkv_a0, rwkv_a_up, rwkv_g_up, rwkv_k_k, rwkv_k_a, rwkv_r_k, rwkv_ln_w, rwkv_ln_b, w_branch, w_out, ffn_w_gate, ffn_w_up, ffn_w_down):
    batch, s, d = x.shape
    depth = w_in.shape[0]
    bw = lru_conv_w.shape[2]
    total = _layout(d, bw)["total"]
    tiles = _tiles(s)
    big = dict(w_in_perm=_permute_w_in(w_in, d, bw, total),
               w_branch=w_branch.astype(BF16), w_out=w_out.astype(BF16),
               ffn_w_gate=ffn_w_gate.astype(BF16), ffn_w_up=ffn_w_up.astype(BF16),
               ffn_w_down=ffn_w_down.astype(BF16))
    outs = []
    for b in range(batch):
        xb = x.reshape(s, d) if batch == 1 else x[b]
        for l in range(depth):
            lp = dict(
                norm_mix_pre=norm_mix_pre[l], norm_mix_post=norm_mix_post[l],
                norm_ffn_pre=norm_ffn_pre[l], norm_ffn_post=norm_ffn_post[l],
                lru_conv_w=lru_conv_w[l], lru_conv_b=lru_conv_b[l],
                lru_wax=_block_diag_gates(lru_w_a[l], lru_w_x[l], 2 * LANES // LRU_BLOCK),
                lru_b_a=lru_b_a[l], lru_b_x=lru_b_x[l], lru_lambda=lru_lambda[l],
                gdn_conv_w=gdn_conv_w[l], gdn_a_log=gdn_a_log[l], gdn_dt_bias=gdn_dt_bias[l],
                gdn_norm_w=gdn_norm_w[l],
                rwkv_mu=rwkv_mu[l], rwkv_w0=rwkv_w0[l], rwkv_w_up=rwkv_w_up[l], rwkv_a0=rwkv_a0[l],
                rwkv_a_up=rwkv_a_up[l], rwkv_g_up=rwkv_g_up[l], rwkv_k_k=rwkv_k_k[l], rwkv_k_a=rwkv_k_a[l],
                rwkv_r_k=rwkv_r_k[l], rwkv_ln_w=rwkv_ln_w[l], rwkv_ln_b=rwkv_ln_b[l])
            xb = _layer(xb, lp, big, l, tiles)
        outs.append(xb)
    return outs[0].reshape(1, s, d) if batch == 1 else jnp.stack(outs, axis=0)
```

```python
import functools

import jax
import jax.numpy as jnp
import numpy as np
from jax import lax
from jax.experimental import pallas as pl
from jax.experimental.pallas import tpu as pltpu

F32 = jnp.float32
BF16 = jnp.bfloat16

EPS = 1e-6
LRU_C = 8.0
LRU_BLOCK = 64
GDN_HEADS = 8
GDN_HEAD_DIM = 128
RWKV_HEAD = 64
RWKV_W_RANK = 64
RWKV_A_RANK = 64
RWKV_G_RANK = 160
RWKV_GN_EPS = 64e-5
N_BRANCH = 3
CONV_W = 4

CHUNK = 64
LANES = 128
SUBLANES = 8
VMEM_LIMIT = 56 * 1024 * 1024

SMALL_W = 512


def _layout(d_model, branch_w):
    off = {}
    c = 0
    for name, width in (("gates", N_BRANCH * d_model), ("lru_x", branch_w), ("lru_g", branch_w),
                        ("gdn_q", branch_w), ("gdn_k", branch_w), ("gdn_v", branch_w), ("gdn_z", branch_w),
                        ("rw_r", branch_w), ("rw_k", branch_w), ("rw_v", branch_w), ("small", SMALL_W)):
        off[name] = c
        c += width
    off["total"] = c
    return off


SM_WLO = 0
SM_ALO = SM_WLO + RWKV_W_RANK
SM_GLO = SM_ALO + RWKV_A_RANK
SM_BETA = SM_GLO + RWKV_G_RANK
SM_ALPHA = SM_BETA + GDN_HEADS
SM_USED = SM_ALPHA + GDN_HEADS
SM_GLO_PAD = 256
SM_PAR = (SM_BETA // LANES) * LANES
assert SM_PAR <= SM_BETA and SM_USED <= SM_PAR + LANES


def _params(*sem):
    return pltpu.CompilerParams(dimension_semantics=sem, vmem_limit_bytes=VMEM_LIMIT)


def _dot(a, b):
    return jnp.dot(a, b, preferred_element_type=F32)


def _dot_nt(a, b):
    return lax.dot_general(a, b, (((1,), (1,)), ((), ())), preferred_element_type=F32)


def _dot_tn(a, b):
    return lax.dot_general(a, b, (((0,), (0,)), ((), ())), preferred_element_type=F32)


def _softplus(x):
    return jnp.maximum(x, 0.0) + jnp.log1p(jnp.exp(-jnp.abs(x)))


def _sigmoid(x):
    return 1.0 / (1.0 + jnp.exp(-x))


def _silu(x):
    return x * _sigmoid(x)


def _gelu_tanh(x):
    c = np.float32(np.sqrt(2.0 / np.pi))
    return 0.5 * x * (1.0 + jnp.tanh(c * (x + 0.044715 * (x * x * x))))


def _tri_masks(n):
    row = lax.broadcasted_iota(jnp.int32, (n, n), 0)
    col = lax.broadcasted_iota(jnp.int32, (n, n), 1)
    return row >= col, row > col, row == col


def _neumann_solve(nils, xs):
    steps = int(np.log2(CHUNK))
    for it in range(steps):
        xs = [x + _dot(p, x) for p, x in zip(nils, xs)]
        if it + 1 < steps:
            nils = [_dot(p, p) for p in nils]
    return xs


def _seg_cumsum(x, seg):
    row = lax.broadcasted_iota(jnp.int32, x.shape, 0)
    pos = row & (seg - 1)
    k = 1
    while k < seg:
        x = jnp.where(pos >= k, x + pltpu.roll(x, k, axis=0), x)
        k *= 2
    return x


def _stage_rows(buf_ref, x):
    buf_ref[SUBLANES:, :] = x


def _rows_before(buf_ref, t, k):
    return buf_ref[pl.ds(SUBLANES - k, t), :]


def _keep_tail(buf_ref, t):
    buf_ref[0:SUBLANES, :] = buf_ref[pl.ds(t, SUBLANES), :]


def _zero_tail(buf_ref):
    buf_ref[0:SUBLANES, :] = jnp.zeros((SUBLANES, buf_ref.shape[1]), buf_ref.dtype)


def _norm_matmul_kernel(x_ref, nw_ref, w_ref, o_ref, h_ref):
    @pl.when(pl.program_id(1) == 0)
    def _():
        x = x_ref[...]
        ms = jnp.mean(x * x, axis=-1, keepdims=True)
        h_ref[...] = (x * lax.rsqrt(ms + EPS) * nw_ref[...]).astype(BF16)

    o_ref[...] = _dot(h_ref[...], w_ref[...])


def _norm_matmul(x, nw, w, layer, tm, tn):
    s, d = x.shape
    n = w.shape[2]
    return pl.pallas_call(
        _norm_matmul_kernel,
        grid=(s // tm, n // tn),
        in_specs=[pl.BlockSpec((tm, d), lambda i, j: (i, 0)),
                  pl.BlockSpec((1, d), lambda i, j: (0, 0)),
                  pl.BlockSpec((None, d, tn), lambda i, j: (layer, 0, j))],
        out_specs=pl.BlockSpec((tm, tn), lambda i, j: (i, j)),
        out_shape=jax.ShapeDtypeStruct((s, n), F32),
        scratch_shapes=[pltpu.VMEM((tm, d), BF16)],
        compiler_params=_params("arbitrary", "arbitrary"),
    )(x, nw, w)


def _lru_kernel(x_ref, g_ref, cw_ref, cb_ref, wax_ref, ba_ref, bx_ref, lam_ref, o_ref, tail_ref, hc_ref):
    t_blk = x_ref.shape[0]
    width = x_ref.shape[1]

    @pl.when(pl.program_id(0) == 0)
    def _():
        _zero_tail(tail_ref)
        hc_ref[...] = jnp.zeros_like(hc_ref)

    x = x_ref[...]
    cw = cw_ref[...]
    _stage_rows(tail_ref, x)
    xc = x * cw[3:4] + cb_ref[...]
    for k in range(1, CONV_W):
        xc = xc + _rows_before(tail_ref, t_blk, k) * cw[CONV_W - 1 - k:CONV_W - k]
    _keep_tail(tail_ref, t_blk)

    nb = wax_ref.shape[0]
    gw = width // nb
    r_parts, i_parts = [], []
    for b in range(nb):
        ri = _dot(xc[:, gw * b:gw * (b + 1)], wax_ref[b])
        r_parts.append(ri[:, :gw])
        i_parts.append(ri[:, gw:])
    r = _sigmoid(jnp.concatenate(r_parts, axis=1) + ba_ref[...])
    i = _sigmoid(jnp.concatenate(i_parts, axis=1) + bx_ref[...])
    log_a = (-LRU_C) * r * _softplus(-lam_ref[...])
    a = jnp.exp(log_a)
    u = jnp.sqrt(1.0 - jnp.exp(2.0 * log_a)) * (i * xc)

    row = lax.broadcasted_iota(jnp.int32, a.shape, 0)
    k = 1
    while k < t_blk:
        m = row >= k
        u = jnp.where(m, a * pltpu.roll(u, k, axis=0) + u, u)
        a = jnp.where(m, a * pltpu.roll(a, k, axis=0), a)
        k *= 2
    h = u + a * hc_ref[0:1, :]
    hc_ref[...] = jnp.broadcast_to(h[t_blk - 1:t_blk, :], hc_ref.shape)
    o_ref[...] = _gelu_tanh(g_ref[...]) * h


def _lru_branch(p, off, cw, cb, wax, ba, bx, lam, t_blk):
    s = p.shape[0]
    width = cw.shape[1]
    xb = off["lru_x"] // width
    gb = off["lru_g"] // width
    vec = pl.BlockSpec((1, width), lambda t: (0, 0))
    return pl.pallas_call(
        _lru_kernel,
        grid=(s // t_blk,),
        in_specs=[pl.BlockSpec((t_blk, width), lambda t: (t, xb)),
                  pl.BlockSpec((t_blk, width), lambda t: (t, gb)),
                  pl.BlockSpec((CONV_W, width), lambda t: (0, 0)),
                  vec,
                  pl.BlockSpec(wax.shape, lambda t: (0, 0, 0)),
                  vec, vec, vec],
        out_specs=pl.BlockSpec((t_blk, width), lambda t: (t, 0)),
        out_shape=jax.ShapeDtypeStruct((s, width), F32),
        scratch_shapes=[pltpu.VMEM((t_blk + SUBLANES, width), F32), pltpu.VMEM((SUBLANES, width), F32)],
        compiler_params=_params("arbitrary"),
    )(p, p, cw, cb, wax, ba, bx, lam)


def _gdn_kernel(q_ref, k_ref, v_ref, z_ref, sm_ref, cwq_ref, cwk_ref, cwv_ref, alog_ref, dtb_ref, nw_ref,
                o_ref, tq_ref, tk_ref, tv_ref, st_ref):
    t_blk = q_ref.shape[0]
    hd = GDN_HEAD_DIM
    nh = q_ref.shape[1] // hd
    n_chunks = t_blk // CHUNK
    head0 = pl.program_id(0) * nh

    @pl.when(pl.program_id(1) == 0)
    def _():
        _zero_tail(tq_ref)
        _zero_tail(tk_ref)
        _zero_tail(tv_ref)
        st_ref[...] = jnp.zeros_like(st_ref)

    def conv_silu(x_ref, tail_ref, cw_ref):
        x = x_ref[...]
        cw = cw_ref[...]
        _stage_rows(tail_ref, x)
        y = x * cw[3:4]
        for k in range(1, CONV_W):
            y = y + _rows_before(tail_ref, t_blk, k) * cw[CONV_W - 1 - k:CONV_W - k]
        _keep_tail(tail_ref, t_blk)
        return _silu(y)

    q_all = conv_silu(q_ref, tq_ref, cwq_ref)
    k_all = conv_silu(k_ref, tk_ref, cwk_ref)
    v_all = conv_silu(v_ref, tv_ref, cwv_ref)

    par = sm_ref[...]
    lane = lax.broadcasted_iota(jnp.int32, (1, LANES), 1)
    beta_all = _sigmoid(par)
    gc_all = _seg_cumsum(-jnp.exp(alog_ref[...]) * _softplus(par + dtb_ref[...]), CHUNK)
    causal, strict, eye = _tri_masks(CHUNK)

    qs, ks, vs, betas, gcs, egcs = [], [], [], [], [], []
    for h in range(nh):
        hs = slice(h * hd, (h + 1) * hd)
        q = q_all[:, hs]
        k = k_all[:, hs]
        qs.append(q * lax.rsqrt(jnp.sum(q * q, axis=-1, keepdims=True) + EPS) * (hd ** -0.5))
        ks.append(k * lax.rsqrt(jnp.sum(k * k, axis=-1, keepdims=True) + EPS))
        vs.append(v_all[:, hs])
        beta = jnp.sum(jnp.where(lane == SM_BETA - SM_PAR + head0 + h, beta_all, 0.0), axis=1, keepdims=True)
        gc = jnp.sum(jnp.where(lane == SM_ALPHA - SM_PAR + head0 + h, gc_all, 0.0), axis=1, keepdims=True)
        betas.append(jnp.broadcast_to(beta, (t_blk, hd)))
        gc = jnp.broadcast_to(gc, (t_blk, hd))
        gcs.append(gc)
        egcs.append(jnp.exp(gc))

    items = [(h, c) for h in range(nh) for c in range(n_chunks)]

    def rows(arrs, h, c):
        return arrs[h][c * CHUNK:(c + 1) * CHUNK]

    kq = [_dot_nt(jnp.concatenate([rows(ks, h, c), rows(qs, h, c)], axis=0), rows(ks, h, c)) for h, c in items]
    nil, attn, xs = [], [], []
    for (h, c), m in zip(items, kq):
        gcc, bc, kc = rows(gcs, h, c), rows(betas, h, c), rows(ks, h, c)
        g_col = gcc[:, :CHUNK]
        g_row = jnp.sum(jnp.where(eye, g_col, 0.0), axis=0, keepdims=True)
        decay = jnp.exp(jnp.where(causal, g_col - g_row, 0.0))
        nil.append(jnp.where(strict, -(bc[:, :CHUNK] * m[:CHUNK] * decay), 0.0))
        attn.append(jnp.where(causal, m[CHUNK:] * decay, 0.0))
        xs.append(jnp.concatenate([bc * rows(vs, h, c), bc * rows(egcs, h, c) * kc], axis=1))
    xs = _neumann_solve(nil, xs)
    g_last = [rows(gcs, h, c)[CHUNK - 1:CHUNK, :] for h, c in items]
    tn = [_dot_tn(rows(ks, h, c) * jnp.exp(gl - rows(gcs, h, c)), x) for (h, c), gl, x in zip(items, g_last, xs)]
    ax = [_dot(a_, x) for a_, x in zip(attn, xs)]
    q_eff = [rows(qs, h, c) * rows(egcs, h, c) - a_[:, hd:] for (h, c), a_ in zip(items, ax)]
    lhs = [jnp.concatenate([t_[:, hd:], q_], axis=0) for t_, q_ in zip(tn, q_eff)]
    e_last = [jnp.exp(gl) for gl in g_last]

    states = [st_ref[h] for h in range(nh)]
    outs = [[] for _ in range(nh)]
    for c in range(n_chunks):
        res = [_dot(lhs[h * n_chunks + c], states[h]) for h in range(nh)]
        for h in range(nh):
            i = h * n_chunks + c
            states[h] = states[h] * e_last[i] - res[h][:hd] + tn[i][:, :hd]
            outs[h].append(res[h][hd:] + ax[i][:, :hd])
    for h in range(nh):
        st_ref[h] = states[h]

    z = z_ref[...]
    for h in range(nh):
        o = jnp.concatenate(outs[h], axis=0)
        o = o * lax.rsqrt(jnp.mean(o * o, axis=-1, keepdims=True) + EPS) * nw_ref[...]
        o_ref[:, h * hd:(h + 1) * hd] = o * _silu(z[:, h * hd:(h + 1) * hd])


def _gdn_branch(p, off, conv_w, a_log, dt_bias, norm_w, t_blk, heads_per_step):
    s = p.shape[0]
    hd = GDN_HEAD_DIM
    nh = GDN_HEADS
    bwid = heads_per_step * hd
    ngrp = nh // heads_per_step
    qb, kb, vb, zb = (off[n] // bwid for n in ("gdn_q", "gdn_k", "gdn_v", "gdn_z"))
    parb = (off["small"] + SM_PAR) // LANES
    blk = lambda base: pl.BlockSpec((t_blk, bwid), lambda h, t: (t, base + h))
    cwb = lambda base: pl.BlockSpec((CONV_W, bwid), lambda h, t: (0, base + h))
    lane_row = pl.BlockSpec((1, LANES), lambda h, t: (0, 0))
    at_alpha = lambda v: jnp.pad(v, ((0, 0), (SM_ALPHA - SM_PAR, LANES - (SM_ALPHA - SM_PAR) - nh)))
    return pl.pallas_call(
        _gdn_kernel,
        grid=(ngrp, s // t_blk),
        in_specs=[blk(qb), blk(kb), blk(vb), blk(zb),
                  pl.BlockSpec((t_blk, LANES), lambda h, t: (t, parb)),
                  cwb(0), cwb(ngrp), cwb(2 * ngrp),
                  lane_row, lane_row,
                  pl.BlockSpec((1, hd), lambda h, t: (0, 0))],
        out_specs=pl.BlockSpec((t_blk, bwid), lambda h, t: (t, h)),
        out_shape=jax.ShapeDtypeStruct((s, nh * hd), F32),
        scratch_shapes=([pltpu.VMEM((t_blk + SUBLANES, bwid), F32)] * 3
                        + [pltpu.VMEM((heads_per_step, hd, hd), F32)]),
        compiler_params=_params("arbitrary", "arbitrary"),
    )(p, p, p, p, p, conv_w, conv_w, conv_w, at_alpha(a_log), at_alpha(dt_bias), norm_w)


def _rwkv_kernel(r_ref, k_ref, v_ref, sm_ref, mur_ref, muk_ref, muv_ref, mus_ref, w0_ref, a0_ref, kk_ref, ka_ref,
                 rk_ref, lnw_ref, lnb_ref, wup_ref, aup_ref, gup_ref,
                 o_ref, tr_ref, tk_ref, tv_ref, ts_ref, st_ref):
    t_blk, pw = r_ref.shape
    hd = RWKV_HEAD
    nh = pw // hd

    @pl.when(pl.program_id(1) == 0)
    def _():
        _zero_tail(tr_ref)
        _zero_tail(tk_ref)
        _zero_tail(tv_ref)
        _zero_tail(ts_ref)
        st_ref[...] = jnp.zeros_like(st_ref)

    def token_shift(x_ref, tail_ref, mu_ref):
        x = x_ref[...]
        _stage_rows(tail_ref, x)
        prev = _rows_before(tail_ref, t_blk, 1)
        _keep_tail(tail_ref, t_blk)
        return x + (prev - x) * mu_ref[...]

    r = token_shift(r_ref, tr_ref, mur_ref)
    k = token_shift(k_ref, tk_ref, muk_ref)
    v = token_shift(v_ref, tv_ref, muv_ref)
    sm = token_shift(sm_ref, ts_ref, mus_ref)

    lo = sm[:, SM_WLO:SM_WLO + LANES]
    w = -_softplus(-(w0_ref[...] + _dot(jnp.tanh(lo), wup_ref[...]))) - 0.5
    a = _sigmoid(a0_ref[...] + _dot(lo, aup_ref[...]))
    g = _dot(_sigmoid(sm[:, SM_GLO:SM_GLO + SM_GLO_PAD]), gup_ref[...])
    log_dec = -jnp.exp(w)

    same_head = (lax.broadcasted_iota(jnp.int32, (pw, pw), 0) // hd) == (lax.broadcasted_iota(jnp.int32, (pw, pw), 1) // hd)
    ones_bd = jnp.where(same_head, 1.0, 0.0).astype(BF16)

    def head_sum(x):
        hi = x.astype(BF16)
        lo = (x - hi.astype(F32)).astype(BF16)
        return _dot(hi, ones_bd) + _dot(lo, ones_bd)

    kraw = k * kk_ref[...]
    kk = kraw * lax.rsqrt(head_sum(kraw * kraw) + EPS)
    k2 = k * (1.0 + (a - 1.0) * ka_ref[...])

    gcs = _seg_cumsum(log_dec, CHUNK)
    e_inc = jnp.exp(gcs)
    e_exc = jnp.exp(gcs - log_dec)
    e_neg = jnp.exp(-gcs)
    xa = -kk * e_exc
    xr = r * e_inc
    yb = kk * a * e_neg
    yk = k2 * e_neg

    n_chunks = t_blk // CHUNK
    npair = pw // LANES
    c2 = 2 * CHUNK
    lane1 = lax.broadcasted_iota(jnp.int32, (1, LANES), 1)
    own = [lane1 < hd, lane1 >= hd]
    causal, strict, _ = _tri_masks(CHUNK)
    r2 = lax.broadcasted_iota(jnp.int32, (c2, c2), 0)
    j2 = lax.broadcasted_iota(jnp.int32, (c2, c2), 1)
    mask2 = (r2 & (CHUNK - 1)) - (j2 & (CHUNK - 1)) + jnp.where(r2 < CHUNK, 0, 1) > 0
    eye2 = r2 == j2
    zeros_c = jnp.zeros((CHUNK, LANES), F32)

    def grp(arr, pp, c):
        return arr[c * CHUNK:(c + 1) * CHUNK, pp * LANES:(pp + 1) * LANES]

    v_sw = [pltpu.roll(v[:, pp * LANES:(pp + 1) * LANES], hd, axis=1) for pp in range(npair)]
    groups = [(pp, c) for pp in range(npair) for c in range(n_chunks)]
    items = [(pp, c, hh) for pp, c in groups for hh in range(2)]

    xcat = {g_: jnp.concatenate([grp(xa, *g_), grp(xr, *g_)], axis=0) for g_ in groups}
    ycat = {g_: jnp.concatenate([grp(yb, *g_), grp(yk, *g_)], axis=0) for g_ in groups}
    d_row = {(pp, c): e_inc[(c + 1) * CHUNK - 1:(c + 1) * CHUNK, pp * LANES:(pp + 1) * LANES] for pp, c in groups}
    ms = [_dot_nt(xcat[(pp, c)], jnp.where(own[hh], ycat[(pp, c)], 0.0)) for pp, c, hh in items]
    y_t = [jnp.where(own[hh], ycat[(pp, c)] * d_row[(pp, c)], 0.0).T for pp, c, hh in items]
    nil = [jnp.where(strict, m[:CHUNK, :CHUNK], 0.0) for m in ms]
    a_rb = [jnp.where(causal, m[CHUNK:, :CHUNK], 0.0) for m in ms]
    v_oth = [jnp.where(own[hh], 0.0, v_sw[pp][c * CHUNK:(c + 1) * CHUNK]) for pp, c, hh in items]
    avk = [_dot(jnp.where(mask2, m, 0.0), jnp.concatenate([zeros_c, vo], axis=0)) for m, vo in zip(ms, v_oth)]
    xs = [jnp.where(own[hh], grp(xa, pp, c), av[:CHUNK]) for av, (pp, c, hh) in zip(avk, items)]
    xs = _neumann_solve(nil, xs)
    tt = [_dot(yt, jnp.concatenate([x, vo], axis=0)) for yt, x, vo in zip(y_t, xs, v_oth)]
    rx = [_dot(a_, x) for a_, x in zip(a_rb, xs)]
    q_eff = [jnp.where(own[hh], grp(xr, pp, c) + r_, 0.0) for r_, (pp, c, hh) in zip(rx, items)]
    o_c = [r_ + av[CHUNK:] for r_, av in zip(rx, avk)]
    lhs, h_c, o_cc, d_col = {}, {}, {}, {}
    for gi, g_ in enumerate(groups):
        i0, i1 = 2 * gi, 2 * gi + 1
        lhs[g_] = jnp.concatenate([jnp.where(own[0], tt[i0], tt[i1]), q_eff[i0], q_eff[i1]], axis=0)
        h_c[g_] = jnp.where(own[0], tt[i1], tt[i0])
        o_cc[g_] = jnp.where(own[0], o_c[i1], o_c[i0])
        d_col[g_] = jnp.sum(jnp.where(eye2, d_row[g_], 0.0), axis=1, keepdims=True)

    states = [st_ref[pp] for pp in range(npair)]
    outs = [[] for _ in range(npair)]
    for c in range(n_chunks):
        res = [_dot(lhs[(pp, c)], states[pp]) for pp in range(npair)]
        for pp in range(npair):
            states[pp] = d_col[(pp, c)] * states[pp] + res[pp][:c2] + h_c[(pp, c)]
            outs[pp].append(res[pp][c2:c2 + CHUNK] + res[pp][c2 + CHUNK:] + o_cc[(pp, c)])
    for pp in range(npair):
        st_ref[pp] = states[pp]
    out = jnp.concatenate([pltpu.roll(jnp.concatenate(o, axis=0), hd, axis=1) for o in outs], axis=1)

    inv_n = 1.0 / hd
    mean = head_sum(out) * inv_n
    cen = out - mean
    var = head_sum(cen * cen) * inv_n
    gn = cen * lax.rsqrt(var + RWKV_GN_EPS) * lnw_ref[...] + lnb_ref[...]
    bonus = head_sum(r * k2 * rk_ref[...]) * v
    o_ref[...] = (gn + bonus) * g


def _rwkv_branch(p, off, mu_r, mu_k, mu_v, mu_s, w0, a0, k_k, k_a, r_k, ln_w, ln_b, w_up, a_up, g_up, t_blk,
                 heads_per_step):
    s = p.shape[0]
    pw = heads_per_step * RWKV_HEAD
    width = w0.shape[1]
    ngrp = width // pw
    rb, kb, vb = (off[n] // pw for n in ("rw_r", "rw_k", "rw_v"))
    smb = off["small"] // SMALL_W
    blk = lambda base: pl.BlockSpec((t_blk, pw), lambda h, t: (t, base + h))
    vec = pl.BlockSpec((1, pw), lambda h, t: (0, h))
    return pl.pallas_call(
        _rwkv_kernel,
        grid=(ngrp, s // t_blk),
        in_specs=[blk(rb), blk(kb), blk(vb),
                  pl.BlockSpec((t_blk, SMALL_W), lambda h, t: (t, smb)),
                  vec, vec, vec,
                  pl.BlockSpec((1, SMALL_W), lambda h, t: (0, 0)),
                  vec, vec, vec, vec, vec, vec, vec,
                  pl.BlockSpec((LANES, pw), lambda h, t: (0, h)),
                  pl.BlockSpec((LANES, pw), lambda h, t: (0, h)),
                  pl.BlockSpec((SM_GLO_PAD, pw), lambda h, t: (0, h))],
        out_specs=pl.BlockSpec((t_blk, pw), lambda h, t: (t, h)),
        out_shape=jax.ShapeDtypeStruct((s, width), F32),
        scratch_shapes=([pltpu.VMEM((t_blk + SUBLANES, pw), F32)] * 3
                        + [pltpu.VMEM((t_blk + SUBLANES, SMALL_W), F32),
                           pltpu.VMEM((pw // LANES, LANES, LANES), F32)]),
        compiler_params=_params("arbitrary", "arbitrary"),
    )(p, p, p, p, mu_r, mu_k, mu_v, mu_s, w0, a0, k_k, k_a, r_k, ln_w, ln_b, w_up, a_up, g_up)


def _merge_out_kernel(ya_ref, yb_ref, yc_ref, ga_ref, gb_ref, gc_ref, wb_ref, wo_ref, nw_ref, x_ref, o_ref):
    acc = None
    for b, (y_ref, g_ref) in enumerate(((ya_ref, ga_ref), (yb_ref, gb_ref), (yc_ref, gc_ref))):
        term = _sigmoid(g_ref[...]) * _dot(y_ref[...].astype(BF16), wb_ref[b])
        acc = term if acc is None else acc + term
    h = _dot(acc.astype(BF16), wo_ref[...])
    ms = jnp.mean(h * h, axis=-1, keepdims=True)
    o_ref[...] = x_ref[...] + h * lax.rsqrt(ms + EPS) * nw_ref[...]


def _merge_out(ya, yb, yc, p, off, w_branch, w_out, layer, nw, x, tm):
    s, bw = ya.shape
    d = x.shape[1]
    gbase = off["gates"] // d
    once = pl.Buffered(1)
    yspec = pl.BlockSpec((tm, bw), lambda i: (i, 0))
    gspec = lambda b: pl.BlockSpec((tm, d), lambda i: (i, gbase + b))
    return pl.pallas_call(
        _merge_out_kernel,
        grid=(s // tm,),
        in_specs=[yspec, yspec, yspec, gspec(0), gspec(1), gspec(2),
                  pl.BlockSpec((None, N_BRANCH, bw, d), lambda i: (layer, 0, 0, 0), pipeline_mode=once),
                  pl.BlockSpec((None, d, d), lambda i: (layer, 0, 0), pipeline_mode=once),
                  pl.BlockSpec((1, d), lambda i: (0, 0)),
                  pl.BlockSpec((tm, d), lambda i: (i, 0))],
        out_specs=pl.BlockSpec((tm, d), lambda i: (i, 0)),
        out_shape=jax.ShapeDtypeStruct((s, d), F32),
        compiler_params=_params("arbitrary"),
    )(ya, yb, yc, p, p, p, w_branch, w_out, nw, x)


def _ffn_kernel(x_ref, npre_ref, npost_ref, wg_ref, wu_ref, wd_ref, o_ref, h_ref, acc_ref):
    kk = pl.program_id(1)

    @pl.when(kk == 0)
    def _():
        x = x_ref[...]
        ms = jnp.mean(x * x, axis=-1, keepdims=True)
        h_ref[...] = (x * lax.rsqrt(ms + EPS) * npre_ref[...]).astype(BF16)
        acc_ref[...] = jnp.zeros_like(acc_ref)

    h = h_ref[...]
    act = (_silu(_dot(h, wg_ref[...])) * _dot(h, wu_ref[...])).astype(BF16)
    acc_ref[...] += _dot(act, wd_ref[...])

    @pl.when(kk == pl.num_programs(1) - 1)
    def _():
        y = acc_ref[...]
        ms = jnp.mean(y * y, axis=-1, keepdims=True)
        o_ref[...] = x_ref[...] + y * lax.rsqrt(ms + EPS) * npost_ref[...]


def _ffn(x, npre, npost, wg, wu, wd, layer, tm, tf):
    s, d = x.shape
    f = wg.shape[2]
    return pl.pallas_call(
        _ffn_kernel,
        grid=(s // tm, f // tf),
        in_specs=[pl.BlockSpec((tm, d), lambda i, k: (i, 0)),
                  pl.BlockSpec((1, d), lambda i, k: (0, 0)),
                  pl.BlockSpec((1, d), lambda i, k: (0, 0)),
                  pl.BlockSpec((None, d, tf), lambda i, k: (layer, 0, k)),
                  pl.BlockSpec((None, d, tf), lambda i, k: (layer, 0, k)),
                  pl.BlockSpec((None, tf, d), lambda i, k: (layer, k, 0))],
        out_specs=pl.BlockSpec((tm, d), lambda i, k: (i, 0)),
        out_shape=jax.ShapeDtypeStruct((s, d), F32),
        scratch_shapes=[pltpu.VMEM((tm, d), BF16), pltpu.VMEM((tm, d), F32)],
        compiler_params=_params("arbitrary", "arbitrary"),
    )(x, npre, npost, wg, wu, wd)


PERMUTE_ROWS = 128
PERMUTE_CHUNK = 1024


def _permute_kernel(plan, narrow, n_in, w_ref, o_ref):
    def window(src, width):
        a0 = (src // LANES) * LANES
        sh = src - a0
        lw = min(-(-(sh + width) // LANES) * LANES, n_in - a0)
        return w_ref[:, a0:a0 + lw][:, sh:sh + width]

    for dst, src, width in plan:
        for c0 in range(0, width, PERMUTE_CHUNK):
            cw = min(PERMUTE_CHUNK, width - c0)
            o_ref[:, dst + c0:dst + c0 + cw] = window(src + c0, cw).astype(BF16)
    lane = lax.broadcasted_iota(jnp.int32, (1, SMALL_W), 1)
    small = jnp.zeros((o_ref.shape[0], SMALL_W), w_ref.dtype)
    for first, width, src in narrow:
        piece = window(src - first, SMALL_W)
        small = jnp.where((lane >= first) & (lane < first + width), piece, small)
    o_ref[:, o_ref.shape[1] - SMALL_W:] = small.astype(BF16)


def _permute_w_in(w_in, d_model, bw, total):
    depth, d, n_in = w_in.shape
    off = _layout(d_model, bw)
    n_aligned = 6 * bw
    src_rw = n_aligned + 2 * GDN_HEADS
    src_lo = src_rw + 3 * bw
    src_gates = src_lo + RWKV_W_RANK + RWKV_A_RANK + RWKV_G_RANK
    plan = ((off["gates"], src_gates, N_BRANCH * d_model),
            (off["lru_x"], 0, n_aligned),
            (off["rw_r"], src_rw, 3 * bw))
    assert src_gates + N_BRANCH * d_model == n_in and off["small"] + SMALL_W == total
    narrow = ((SM_WLO, SM_BETA - SM_WLO, src_lo), (SM_BETA, SM_USED - SM_BETA, n_aligned))
    assert all(src - first >= 0 and src - first + SMALL_W <= n_in for first, _, src in narrow)
    rows = depth * d
    out = pl.pallas_call(
        functools.partial(_permute_kernel, plan, narrow, n_in),
        grid=(rows // PERMUTE_ROWS,),
        in_specs=[pl.BlockSpec((PERMUTE_ROWS, n_in), lambda i: (i, 0))],
        out_specs=pl.BlockSpec((PERMUTE_ROWS, total), lambda i: (i, 0)),
        out_shape=jax.ShapeDtypeStruct((rows, total), BF16),
        compiler_params=_params("arbitrary"),
    )(w_in.astype(BF16).reshape(rows, n_in))
    return out.reshape(depth, d, total)


def _block_diag_gates(w_a, w_x, group):
    nblk, bs, _ = w_a.shape
    ng = nblk // group
    gw = group * bs
    eye = jnp.eye(group, dtype=w_a.dtype)[None, :, None, :, None]

    def bd(w):
        return (w.reshape(ng, group, bs, 1, bs) * eye).reshape(ng, gw, gw)

    return jnp.concatenate([bd(w_a), bd(w_x)], axis=2)


def _pad_rows(w, rows, at):
    return jnp.pad(w, ((at, rows - at - w.shape[0]), (0, 0)))


def _tiles(s):
    return dict(tm_in=min(s, 1024), tn_in=512, t_lru=min(s, 256),
                t_gdn=min(s, 512), hb_gdn=4, t_rwkv=min(s, 512), hb_rwkv=4,
                tm_merge=min(s, 256), tm_ffn=min(s, 512), tf_ffn=512)


def _layer(x, lp, big, layer, tiles):
    s, d = x.shape
    bw = lp["lru_conv_w"].shape[1]
    off = _layout(d, bw)
    row = lambda v: v.reshape(1, -1)

    p = _norm_matmul(x, row(lp["norm_mix_pre"]), big["w_in_perm"], layer, tiles["tm_in"], tiles["tn_in"])

    y_a = _lru_branch(p, off, lp["lru_conv_w"], row(lp["lru_conv_b"]), lp["lru_wax"], row(lp["lru_b_a"]),
                      row(lp["lru_b_x"]), row(lp["lru_lambda"]), tiles["t_lru"])
    y_b = _gdn_branch(p, off, lp["gdn_conv_w"], row(lp["gdn_a_log"]), row(lp["gdn_dt_bias"]),
                      row(lp["gdn_norm_w"]), tiles["t_gdn"], tiles["hb_gdn"])
    mu = lp["rwkv_mu"]
    mu_s = jnp.pad(mu[3 * bw:], (0, SMALL_W - SM_BETA)).reshape(1, SMALL_W)
    y_c = _rwkv_branch(p, off, row(mu[:bw]), row(mu[bw:2 * bw]), row(mu[2 * bw:3 * bw]), mu_s,
                       row(lp["rwkv_w0"]), row(lp["rwkv_a0"]), row(lp["rwkv_k_k"]), row(lp["rwkv_k_a"]),
                       row(lp["rwkv_r_k"]), row(lp["rwkv_ln_w"]), row(lp["rwkv_ln_b"]),
                       _pad_rows(lp["rwkv_w_up"], LANES, SM_WLO), _pad_rows(lp["rwkv_a_up"], LANES, SM_ALO),
                       _pad_rows(lp["rwkv_g_up"], SM_GLO_PAD, 0), tiles["t_rwkv"], tiles["hb_rwkv"])

    x = _merge_out(y_a, y_b, y_c, p, off, big["w_branch"], big["w_out"], layer, row(lp["norm_mix_post"]), x,
                   tiles["tm_merge"])
    x = _ffn(x, row(lp["norm_ffn_pre"]), row(lp["norm_ffn_post"]), big["ffn_w_gate"], big["ffn_w_up"],
             big["ffn_w_down"], layer, tiles["tm_ffn"], tiles["tf_ffn"])
    return x


def kernel(x, norm_mix_pre, norm_mix_post, norm_ffn_pre, norm_ffn_post, w_in, lru_conv_w, lru_conv_b, lru_w_a, lru_b_a, lru_w_x, lru_b_x, lru_lambda, gdn_conv_w, gdn_a_log, gdn_dt_bias, gdn_norm_w, rwkv_mu, rwkv_w0, rwkv_w_up, rwkv_a0, rwkv_a_up, rwkv_g_up, rwkv_k_k, rwkv_k_a, rwkv_r_k, rwkv_ln_w, rwkv_ln_b, w_branch, w_out, ffn_w_gate, ffn_w_up, ffn_w_down):
    batch, s, d = x.shape
    depth = w_in.shape[0]
    bw = lru_conv_w.shape[2]
    total = _layout(d, bw)["total"]
    tiles = _tiles(s)
    big = dict(w_in_perm=_permute_w_in(w_in, d, bw, total),
               w_branch=w_branch.astype(BF16), w_out=w_out.astype(BF16),
               ffn_w_gate=ffn_w_gate.astype(BF16), ffn_w_up=ffn_w_up.astype(BF16),
               ffn_w_down=ffn_w_down.astype(BF16))
    outs = []
    for b in range(batch):
        xb = x.reshape(s, d) if batch == 1 else x[b]
        for l in range(depth):
            lp = dict(
                norm_mix_pre=norm_mix_pre[l], norm_mix_post=norm_mix_post[l],
                norm_ffn_pre=norm_ffn_pre[l], norm_ffn_post=norm_ffn_post[l],
                lru_conv_w=lru_conv_w[l], lru_conv_b=lru_conv_b[l],
                lru_wax=_block_diag_gates(lru_w_a[l], lru_w_x[l], 2 * LANES // LRU_BLOCK),
                lru_b_a=lru_b_a[l], lru_b_x=lru_b_x[l], lru_lambda=lru_lambda[l],
                gdn_conv_w=gdn_conv_w[l], gdn_a_log=gdn_a_log[l], gdn_dt_bias=gdn_dt_bias[l],
                gdn_norm_w=gdn_norm_w[l],
                rwkv_mu=rwkv_mu[l], rwkv_w0=rwkv_w0[l], rwkv_w_up=rwkv_w_up[l], rwkv_a0=rwkv_a0[l],
                rwkv_a_up=rwkv_a_up[l], rwkv_g_up=rwkv_g_up[l], rwkv_k_k=rwkv_k_k[l], rwkv_k_a=rwkv_k_a[l],
                rwkv_r_k=rwkv_r_k[l], rwkv_ln_w=rwkv_ln_w[l], rwkv_ln_b=rwkv_ln_b[l])
            xb = _layer(xb, lp, big, l, tiles)
        outs.append(xb)
    return outs[0].reshape(1, s, d) if batch == 1 else jnp.stack(outs, axis=0)
```
